```python
import math
import jax, jax.numpy as jnp
from jax import lax
import numpy as np

D_MODEL = 1024
BATCH = 8
SEQ = 2048
DEPTH = 4

GRID_W = 64
CTX_LEN = 256
N_MIXERS = 3
Q_BLOCK = 128
ROPE_THETA = 10000.0
LN_EPS = 1e-5
RMS_EPS = 1e-6
NEG_INF = -1e30

A_HEADS = 8
A_KV_HEADS = 2
A_HEAD_DIM = 128
B_HEADS = 16
B_KV_HEADS = 4
B_HEAD_DIM = 64
WINDOW = 128
C_HEADS = 8
C_HEAD_DIM = 64
N_GROUPS = 4
EXPERTS_PER_GROUP = 8
EXPERT_FF = 512
TOP_K = 2

N_A = (DEPTH + 2) // 3
N_B = (DEPTH + 1) // 3
N_C = DEPTH // 3

DEEPNORM_ALPHA = (2.0 * DEPTH) ** 0.25
DEEPNORM_BETA = (8.0 * DEPTH) ** -0.25

kernel_name = 'hybrid_diffusion_backbone'


def layer_norm(x, g, b):
    xf = x.astype(jnp.float32)
    mu = jnp.mean(xf, axis=-1, keepdims=True)
    var = jnp.mean(jnp.square(xf - mu), axis=-1, keepdims=True)
    y = (xf - mu) * lax.rsqrt(var + LN_EPS)
    return (y * g.astype(jnp.float32) + b.astype(jnp.float32)).astype(x.dtype)


def rms_norm(x, g):
    xf = x.astype(jnp.float32)
    y = xf * lax.rsqrt(jnp.mean(jnp.square(xf), axis=-1, keepdims=True) + RMS_EPS)
    return (y * g.astype(jnp.float32)).astype(x.dtype)


def axial_rope(n_tokens, head_dim):
    rows = n_tokens // GRID_W
    row = jnp.broadcast_to(jnp.arange(rows, dtype=jnp.float32)[:, None], (rows, GRID_W)).reshape(-1)
    col = jnp.broadcast_to(jnp.arange(GRID_W, dtype=jnp.float32)[None, :], (rows, GRID_W)).reshape(-1)
    n_freq = head_dim // 4
    inv = ROPE_THETA ** (-jnp.arange(n_freq, dtype=jnp.float32) / n_freq)
    ang = jnp.stack([row[:, None] * inv, col[:, None] * inv], axis=1)
    return jnp.cos(ang), jnp.sin(ang)


def apply_rope(x, cos, sin):
    nf = x.shape[-1] // 4
    xs = x.reshape(x.shape[:-1] + (2, 2, nf))
    x1, x2 = xs[..., 0, :], xs[..., 1, :]
    c, s = cos.astype(x.dtype), sin.astype(x.dtype)
    out = jnp.stack([x1 * c - x2 * s, x1 * s + x2 * c], axis=-2)
    return out.reshape(x.shape)


def to_blocks(x, axis):
    n = x.shape[axis] // Q_BLOCK
    x = x.reshape(x.shape[:axis] + (n, Q_BLOCK) + x.shape[axis + 1:])
    return jnp.moveaxis(x, axis, 0)


def from_blocks(x, axis):
    x = jnp.moveaxis(x, 0, axis)
    return x.reshape(x.shape[:axis] + (-1,) + x.shape[axis + 2:])


def _qkv_gqa(h, w_qkv, n_heads, n_kv, head_dim):
    b, t, _ = h.shape
    g = n_heads // n_kv
    q, k, v = jnp.split(h @ w_qkv, [n_heads * head_dim, (n_heads + n_kv) * head_dim], axis=-1)
    q = q.reshape(b, t, n_kv, g, head_dim).transpose(0, 2, 3, 1, 4)
    k = k.reshape(b, t, n_kv, head_dim).transpose(0, 2, 1, 3)
    v = v.reshape(b, t, n_kv, head_dim).transpose(0, 2, 1, 3)
    return q, k, v


def _merge_gqa(o):
    b, hkv, g, t, d = o.shape
    return o.transpose(0, 3, 1, 2, 4).reshape(b, t, hkv * g * d)


def _gqa_attend(q, k, v, scale):
    s = jnp.einsum('bhgqd,bhkd->bhgqk', q, k).astype(jnp.float32) * scale
    p = jax.nn.softmax(s, axis=-1).astype(v.dtype)
    return jnp.einsum('bhgqk,bhkd->bhgqd', p, v)


def mixer_a(h_lat, h_ctx, w_qkv, q_gain, k_gain, w_o, need_ctx):
    cos, sin = axial_rope(h_lat.shape[1], A_HEAD_DIM)
    scale = A_HEAD_DIM ** -0.5
    q, k, v = _qkv_gqa(h_lat, w_qkv, A_HEADS, A_KV_HEADS, A_HEAD_DIM)
    q = apply_rope(rms_norm(q, q_gain), cos, sin)
    k = apply_rope(rms_norm(k, k_gain), cos, sin)
    qc, kc, vc = _qkv_gqa(h_ctx, w_qkv, A_HEADS, A_KV_HEADS, A_HEAD_DIM)
    qc, kc = rms_norm(qc, q_gain), rms_norm(kc, k_gain)
    k_all = jnp.concatenate([k, kc], axis=2)
    v_all = jnp.concatenate([v, vc], axis=2)
    o = lax.map(lambda qb: _gqa_attend(qb, k_all, v_all, scale), to_blocks(q, 3))
    o_lat = _merge_gqa(from_blocks(o, 3)) @ w_o
    o_ctx = _merge_gqa(_gqa_attend(qc, kc, vc, scale)) @ w_o if need_ctx else None
    return o_lat, o_ctx


def mixer_b(h_lat, h_ctx, w_qkv, sink, w_o, need_ctx):
    s_len = h_lat.shape[1]
    cos, sin = axial_rope(s_len, B_HEAD_DIM)
    scale = B_HEAD_DIM ** -0.5
    g = B_HEADS // B_KV_HEADS
    q, k, v = _qkv_gqa(h_lat, w_qkv, B_HEADS, B_KV_HEADS, B_HEAD_DIM)
    q, k = apply_rope(q, cos, sin), apply_rope(k, cos, sin)
    qc, kc, vc = _qkv_gqa(h_ctx, w_qkv, B_HEADS, B_KV_HEADS, B_HEAD_DIM)
    n_ctx = kc.shape[2]
    span = Q_BLOCK + 2 * WINDOW
    pad = ((0, 0), (0, 0), (WINDOW, WINDOW), (0, 0))
    k_pad, v_pad = jnp.pad(k, pad), jnp.pad(v, pad)
    sink_l = sink.astype(jnp.float32).reshape(1, B_KV_HEADS, g, 1, 1)
    offs_q = jnp.arange(Q_BLOCK)
    offs_k = jnp.arange(span)
    ctx_valid = jnp.ones((Q_BLOCK, n_ctx), dtype=bool)

    def with_sink(s):
        return jnp.concatenate([s, jnp.broadcast_to(sink_l, s.shape[:-1] + (1,))], axis=-1)

    def block(args):
        j, qb = args
        start = j * Q_BLOCK
        kb = lax.dynamic_slice_in_dim(k_pad, start, span, axis=2)
        vb = lax.dynamic_slice_in_dim(v_pad, start, span, axis=2)
        qpos = start + offs_q
        kpos = start - WINDOW + offs_k
        valid = (jnp.abs(qpos[:, None] - kpos[None, :]) <= WINDOW) & ((kpos >= 0) & (kpos < s_len))[None, :]
        valid = jnp.concatenate([valid, ctx_valid], axis=1)
        kk = jnp.concatenate([kb, kc], axis=2)
        vv = jnp.concatenate([vb, vc], axis=2)
        s = jnp.einsum('bhgqd,bhkd->bhgqk', qb, kk).astype(jnp.float32) * scale
        s = jnp.where(valid, s, NEG_INF)
        p = jax.nn.softmax(with_sink(s), axis=-1)[..., :-1].astype(vv.dtype)
        return jnp.einsum('bhgqk,bhkd->bhgqd', p, vv)

    o = lax.map(block, (jnp.arange(s_len // Q_BLOCK), to_blocks(q, 3)))
    o_lat = _merge_gqa(from_blocks(o, 3)) @ w_o
    o_ctx = None
    if need_ctx:
        s = jnp.einsum('bhgqd,bhkd->bhgqk', qc, kc).astype(jnp.float32) * scale
        p = jax.nn.softmax(with_sink(s), axis=-1)[..., :-1].astype(vc.dtype)
        o_ctx = _merge_gqa(jnp.einsum('bhgqk,bhkd->bhgqd', p, vc)) @ w_o
    return o_lat, o_ctx


def _qkv_diff(h, w_qkv):
    b, t, _ = h.shape
    q, k, v = jnp.split(h @ w_qkv, 3, axis=-1)
    q = q.reshape(b, t, C_HEADS, 2, C_HEAD_DIM).transpose(0, 2, 3, 1, 4)
    k = k.reshape(b, t, C_HEADS, 2, C_HEAD_DIM).transpose(0, 2, 3, 1, 4)
    v = v.reshape(b, t, C_HEADS, 2 * C_HEAD_DIM).transpose(0, 2, 1, 3)
    return q, k, v


def _diff_attend(q, k, v, lam, scale):
    s = jnp.einsum('bhiqd,bhikd->bhiqk', q, k).astype(jnp.float32) * scale
    p = jax.nn.softmax(s, axis=-1)
    a = (p[:, :, 0] - lam * p[:, :, 1]).astype(v.dtype)
    return jnp.einsum('bhqk,bhkd->bhqd', a, v)


def _merge_diff(o, subln_gain, lam_init):
    o = rms_norm(o, subln_gain) * (1.0 - lam_init)
    b, h, t, d = o.shape
    return o.transpose(0, 2, 1, 3).reshape(b, t, h * d)


def mixer_c(h_lat, h_ctx, w_qkv, lam_q1, lam_k1, lam_q2, lam_k2, subln_gain, w_o, lam_init, need_ctx):
    cos, sin = axial_rope(h_lat.shape[1], C_HEAD_DIM)
    scale = C_HEAD_DIM ** -0.5
    lam = (jnp.exp(jnp.sum(lam_q1.astype(jnp.float32) * lam_k1.astype(jnp.float32)))
           - jnp.exp(jnp.sum(lam_q2.astype(jnp.float32) * lam_k2.astype(jnp.float32))) + lam_init)
    q, k, v = _qkv_diff(h_lat, w_qkv)
    q, k = apply_rope(q, cos, sin), apply_rope(k, cos, sin)
    qc, kc, vc = _qkv_diff(h_ctx, w_qkv)
    k_all = jnp.concatenate([k, kc], axis=3)
    v_all = jnp.concatenate([v, vc], axis=2)
    o = lax.map(lambda qb: _diff_attend(qb, k_all, v_all, lam, scale), to_blocks(q, 3))
    o_lat = _merge_diff(from_blocks(o, 2), subln_gain, lam_init) @ w_o
    o_ctx = _merge_diff(_diff_attend(qc, kc, vc, lam, scale), subln_gain, lam_init) @ w_o if need_ctx else None
    return o_lat, o_ctx


def hier_moe(h, w_group, b_group, w_expert, b_expert, w_gate, w_up, w_down):
    shp = h.shape
    t = h.reshape(-1, shp[-1])
    g_logits = (t @ w_group + b_group).astype(jnp.float32)
    g_sel = jnp.argmax(g_logits, axis=-1)
    g_prob = jnp.take_along_axis(jax.nn.softmax(g_logits, axis=-1), g_sel[:, None], axis=-1)
    e_logits = (t @ w_expert + b_expert).astype(jnp.float32).reshape(-1, N_GROUPS, EXPERTS_PER_GROUP)
    e_in_group = jnp.take_along_axis(e_logits, g_sel[:, None, None], axis=1)[:, 0]
    top_v, top_i = lax.top_k(e_in_group, TOP_K)
    top_w = jax.nn.softmax(top_v, axis=-1) * g_prob
    e_gate = jnp.sum(jax.nn.one_hot(top_i, EXPERTS_PER_GROUP, dtype=jnp.float32) * top_w[..., None], axis=1)
    combine = (jax.nn.one_hot(g_sel, N_GROUPS, dtype=jnp.float32)[:, :, None] * e_gate[:, None, :]).astype(t.dtype)
    y = jnp.zeros_like(t)
    for g in range(N_GROUPS):
        a = jnp.einsum('nd,edf->nef', t, w_gate[g])
        u = jnp.einsum('nd,edf->nef', t, w_up[g])
        hid = jax.nn.silu(a) * u * combine[:, g, :, None]
        y = y + jnp.einsum('nef,efd->nd', hid, w_down[g])
    return y.reshape(shp)


def setup_inputs(seed: int = 0) -> dict:
    key = jax.random.key(seed)
    ks = iter(jax.random.split(key, 32))
    D = D_MODEL

    def nrm(shape, scale):
        return jax.random.normal(next(ks), shape, jnp.float32) * scale

    a_qkv = (A_HEADS + 2 * A_KV_HEADS) * A_HEAD_DIM
    b_qkv = (B_HEADS + 2 * B_KV_HEADS) * B_HEAD_DIM
    c_qkv = 3 * 2 * C_HEADS * C_HEAD_DIM
    n_exp = N_GROUPS * EXPERTS_PER_GROUP
    return {
        'x': nrm((BATCH, SEQ, D), 1.0),
        'c': nrm((BATCH, D), 1.0),
        'ctx': nrm((BATCH, CTX_LEN, D), 1.0),
        'c_ctx': nrm((D,), 1.0),
        'w_mod': nrm((DEPTH, D, 6 * D), 0.5 * D ** -0.5),
        'b_mod': nrm((DEPTH, 6 * D), 0.02),
        'ln_gain': 1.0 + nrm((DEPTH, 2, D), 0.02),
        'ln_bias': nrm((DEPTH, 2, D), 0.02),
        'a_w_qkv': nrm((N_A, D, a_qkv), D ** -0.5),
        'a_q_gain': 1.0 + nrm((N_A, A_HEAD_DIM), 0.02),
        'a_k_gain': 1.0 + nrm((N_A, A_HEAD_DIM), 0.02),
        'a_w_o': nrm((N_A, A_HEADS * A_HEAD_DIM, D), DEEPNORM_BETA * (A_HEADS * A_HEAD_DIM) ** -0.5),
        'b_w_qkv': nrm((N_B, D, b_qkv), D ** -0.5),
        'b_sink': nrm((N_B, B_HEADS), 0.5),
        'b_w_o': nrm((N_B, B_HEADS * B_HEAD_DIM, D), DEEPNORM_BETA * (B_HEADS * B_HEAD_DIM) ** -0.5),
        'c_w_qkv': nrm((N_C, D, c_qkv), D ** -0.5),
        'c_lam_q1': nrm((N_C, C_HEAD_DIM), 0.1),
        'c_lam_k1': nrm((N_C, C_HEAD_DIM), 0.1),
        'c_lam_q2': nrm((N_C, C_HEAD_DIM), 0.1),
        'c_lam_k2': nrm((N_C, C_HEAD_DIM), 0.1),
        'c_subln_gain': 1.0 + nrm((N_C, 2 * C_HEAD_DIM), 0.02),
        'c_w_o': nrm((N_C, 2 * C_HEADS * C_HEAD_DIM, D), DEEPNORM_BETA * (2 * C_HEADS * C_HEAD_DIM) ** -0.5),
        'moe_w_group': nrm((DEPTH, D, N_GROUPS), D ** -0.5),
        'moe_b_group': nrm((DEPTH, N_GROUPS), 0.01),
        'moe_w_expert': nrm((DEPTH, D, n_exp), D ** -0.5),
        'moe_b_expert': nrm((DEPTH, n_exp), 0.01),
        'moe_w_gate': nrm((DEPTH, N_GROUPS, EXPERTS_PER_GROUP, D, EXPERT_FF), D ** -0.5),
        'moe_w_up': nrm((DEPTH, N_GROUPS, EXPERTS_PER_GROUP, D, EXPERT_FF), D ** -0.5),
        'moe_w_down': nrm((DEPTH, N_GROUPS, EXPERTS_PER_GROUP, EXPERT_FF, D), DEEPNORM_BETA * EXPERT_FF ** -0.5),
    }


def reference(x, c, ctx, c_ctx, w_mod, b_mod, ln_gain, ln_bias,
              a_w_qkv, a_q_gain, a_k_gain, a_w_o,
              b_w_qkv, b_sink, b_w_o,
              c_w_qkv, c_lam_q1, c_lam_k1, c_lam_q2, c_lam_k2, c_subln_gain, c_w_o,
              moe_w_group, moe_b_group, moe_w_expert, moe_b_expert, moe_w_gate, moe_w_up, moe_w_down):
    n_ctx = ctx.shape[1]
    xl, xc = x, ctx
    silu_c = jax.nn.silu(c)
    silu_cc = jax.nn.silu(c_ctx)
    for i in range(DEPTH):
        need_ctx = i < DEPTH - 1
        m_lat = (silu_c @ w_mod[i] + b_mod[i])[:, None, :]
        m_ctx = (silu_cc @ w_mod[i] + b_mod[i])[None, None, :]
        sh1, sc1, g1, sh2, sc2, g2 = jnp.split(m_lat, 6, axis=-1)
        csh1, csc1, cg1, csh2, csc2, cg2 = jnp.split(m_ctx, 6, axis=-1)
        h_lat = xl * (1.0 + sc1) + sh1
        h_ctx = xc * (1.0 + csc1) + csh1
        kind, j = i % N_MIXERS, i // N_MIXERS
        if kind == 0:
            o_lat, o_ctx = mixer_a(h_lat, h_ctx, a_w_qkv[j], a_q_gain[j], a_k_gain[j], a_w_o[j], need_ctx)
        elif kind == 1:
            o_lat, o_ctx = mixer_b(h_lat, h_ctx, b_w_qkv[j], b_sink[j], b_w_o[j], need_ctx)
        else:
            lam_init = 0.8 - 0.6 * math.exp(-0.3 * i)
            o_lat, o_ctx = mixer_c(h_lat, h_ctx, c_w_qkv[j], c_lam_q1[j], c_lam_k1[j], c_lam_q2[j], c_lam_k2[j],
                                   c_subln_gain[j], c_w_o[j], lam_init, need_ctx)
        xl = layer_norm(DEEPNORM_ALPHA * xl + g1 * o_lat, ln_gain[i, 0], ln_bias[i, 0])
        moe_args = (moe_w_group[i], moe_b_group[i], moe_w_expert[i], moe_b_expert[i],
                    moe_w_gate[i], moe_w_up[i], moe_w_down[i])
        if need_ctx:
            xc = layer_norm(DEEPNORM_ALPHA * xc + cg1 * o_ctx, ln_gain[i, 0], ln_bias[i, 0])
            h = jnp.concatenate([xc * (1.0 + csc2) + csh2, xl * (1.0 + sc2) + sh2], axis=1)
            f = hier_moe(h, *moe_args)
            f_ctx, f_lat = f[:, :n_ctx], f[:, n_ctx:]
            xc = layer_norm(DEEPNORM_ALPHA * xc + cg2 * f_ctx, ln_gain[i, 1], ln_bias[i, 1])
        else:
            f_lat = hier_moe(xl * (1.0 + sc2) + sh2, *moe_args)
        xl = layer_norm(DEEPNORM_ALPHA * xl + g2 * f_lat, ln_gain[i, 1], ln_bias[i, 1])
    return xl
```

```python
import functools

import numpy as np
import jax
import jax.numpy as jnp
from jax import lax
from jax.experimental import pallas as pl
from jax.experimental.pallas import tpu as pltpu

GRID_W = 64
ROPE_THETA = 10000.0
LN_EPS = 1e-5
RMS_EPS = 1e-6
NEG_INF = -1e30
N_MIXERS = 3

A_HEADS, A_KV_HEADS, A_HEAD_DIM = 8, 2, 128
B_HEADS, B_KV_HEADS, B_HEAD_DIM = 16, 4, 64
WINDOW = 128
C_HEADS, C_HEAD_DIM = 8, 64
N_GROUPS, EXPERTS_PER_GROUP = 4, 8
N_EXPERTS = N_GROUPS * EXPERTS_PER_GROUP

LANES = 128
SUBLANES = 8
HALF = LANES // 2
TOKEN_BLOCK = 256
EXPERT_TILE = 256
ROUTER_ROWS = 128
VMEM_LIMIT = 56 * 1024 * 1024

F32 = jnp.float32
BF16 = jnp.bfloat16


def _nt_dot(a, b):
    return lax.dot_general(a, b, (((1,), (1,)), ((), ())), preferred_element_type=F32)


def _dot(a, b):
    return jnp.dot(a, b, preferred_element_type=F32)


def _split_bf16(x):
    hi = x.astype(BF16)
    lo = (x - hi.astype(F32)).astype(BF16)
    return hi, lo


def _params(n_axes):
    return pltpu.CompilerParams(dimension_semantics=("arbitrary",) * n_axes, vmem_limit_bytes=VMEM_LIMIT)


def _slot_layout(n_freq):
    n_sub = LANES // (4 * n_freq)
    perm = np.zeros(LANES, np.int32)
    axis_of = np.zeros(LANES, np.int32)
    freq_of = np.zeros(LANES, np.int32)
    half_of = np.zeros(LANES, np.int32)
    sub_of = np.zeros(LANES, np.int32)
    for half in range(2):
        for sub in range(n_sub):
            for axis in range(2):
                for f in range(n_freq):
                    l = half * HALF + sub * 2 * n_freq + axis * n_freq + f
                    perm[l] = sub * 4 * n_freq + axis * 2 * n_freq + half * n_freq + f
                    axis_of[l], freq_of[l], half_of[l], sub_of[l] = axis, f, half, sub
    return perm, axis_of, freq_of, half_of, sub_of


def _rope_tables(seq, ctx, n_freq, q_scale):
    _, axis_of, freq_of, half_of, _ = _slot_layout(n_freq)
    rows = seq // GRID_W
    row = jnp.broadcast_to(jnp.arange(rows, dtype=F32)[:, None], (rows, GRID_W)).reshape(-1)
    col = jnp.broadcast_to(jnp.arange(GRID_W, dtype=F32)[None, :], (rows, GRID_W)).reshape(-1)
    inv = ROPE_THETA ** (-jnp.arange(n_freq, dtype=F32) / n_freq)
    ang = jnp.stack([row[:, None] * inv, col[:, None] * inv], axis=1)
    cos, sin = jnp.cos(ang), jnp.sin(ang)
    cos_l = cos[:, axis_of, freq_of]
    sin_l = sin[:, axis_of, freq_of] * jnp.asarray(np.where(half_of == 0, -1.0, 1.0), F32)
    cos_l = jnp.concatenate([jnp.ones((ctx, LANES), F32), cos_l], axis=0)
    sin_l = jnp.concatenate([jnp.zeros((ctx, LANES), F32), sin_l], axis=0)
    return cos_l * q_scale, sin_l * q_scale, cos_l, sin_l


def _mod_kernel(c_ref, w_ref, b_ref, o_ref):
    c = c_ref[...]
    s = c * (1.0 / (1.0 + jnp.exp(-c)))
    s_hi, s_lo = _split_bf16(s)
    w_hi, w_lo = _split_bf16(w_ref[...])
    o_ref[...] = _dot(s_hi, w_hi) + _dot(s_lo, w_hi) + _dot(s_hi, w_lo) + b_ref[...]


def _modulation(cc, w_mod, b_mod):
    depth, d, n = w_mod.shape
    rows = cc.shape[0]
    tn = n // 4
    return pl.pallas_call(
        _mod_kernel,
        grid=(depth, n // tn),
        in_specs=[
            pl.BlockSpec((rows, d), lambda l, j: (0, 0)),
            pl.BlockSpec((None, d, tn), lambda l, j: (l, 0, j)),
            pl.BlockSpec((None, 1, tn), lambda l, j: (l, 0, j)),
        ],
        out_specs=pl.BlockSpec((None, rows, tn), lambda l, j: (l, 0, j)),
        out_shape=jax.ShapeDtypeStruct((depth, rows, n), F32),
        compiler_params=_params(2),
        name="modulation",
    )(cc, w_mod, b_mod.reshape(depth, 1, n))


def _qkv_kernel(*refs, n_q, n_k, rms):
    if rms:
        x_ref, mod_ref, w_ref, cq_ref, sq_ref, ck_ref, sk_ref, gq_ref, gk_ref, q_ref, k_ref, v_ref = refs
    else:
        x_ref, mod_ref, w_ref, cq_ref, sq_ref, ck_ref, sk_ref, q_ref, k_ref, v_ref = refs
        gq_ref = gk_ref = None
    sh, sc = mod_ref[0:1, :], mod_ref[1:2, :]
    h = (x_ref[...] * (1.0 + sc) + sh).astype(BF16)

    def rope_slots(y, cos, sin, gain_ref, out_ref, n_slots):
        for j in range(n_slots):
            yj = y[:, j * LANES:(j + 1) * LANES]
            if gain_ref is not None:
                ms = jnp.mean(yj * yj, axis=-1, keepdims=True)
                yj = yj * lax.rsqrt(ms + RMS_EPS) * gain_ref[...]
            out = yj * cos + pltpu.roll(yj, HALF, 1) * sin
            out_ref[:, j * LANES:(j + 1) * LANES] = out.astype(BF16)

    nq, nk = n_q * LANES, n_k * LANES
    rope_slots(_dot(h, w_ref[:, 0:nq]), cq_ref[...], sq_ref[...], gq_ref, q_ref, n_q)
    rope_slots(_dot(h, w_ref[:, nq:nq + nk]), ck_ref[...], sk_ref[...], gk_ref, k_ref, n_k)
    v_ref[...] = _dot(h, w_ref[:, nq + nk:]).astype(BF16)


def _qkv_call(x, mod, w, tables, gains, n_q, n_k, n_v):
    bsz, t_len, d = x.shape
    n_t = t_len // TOKEN_BLOCK
    n_rows = mod.shape[0]
    rms = gains is not None
    tok = lambda b, t: (b, t, 0)
    tab_spec = pl.BlockSpec((TOKEN_BLOCK, LANES), lambda b, t: (t, 0))
    in_specs = [
        pl.BlockSpec((None, TOKEN_BLOCK, d), tok),
        pl.BlockSpec((None, 6, d), lambda b, t: (jnp.where(t == 0, bsz, b), 0, 0)),
        pl.BlockSpec(w.shape, lambda b, t: (0, 0)),
        tab_spec, tab_spec, tab_spec, tab_spec,
    ]
    args = [x, mod, w, *tables]
    if rms:
        in_specs += [pl.BlockSpec((1, LANES), lambda b, t: (0, 0))] * 2
        args += list(gains)
    del n_rows
    return pl.pallas_call(
        functools.partial(_qkv_kernel, n_q=n_q, n_k=n_k, rms=rms),
        grid=(bsz, n_t),
        in_specs=in_specs,
        out_specs=[pl.BlockSpec((None, TOKEN_BLOCK, n * LANES), tok) for n in (n_q, n_k, n_v)],
        out_shape=[jax.ShapeDtypeStruct((bsz, t_len, n * LANES), BF16) for n in (n_q, n_k, n_v)],
        compiler_params=_params(2),
        name="qkv_rope",
    )(*args)


def _attn_a_kernel(q_ref, k_ref, v_ref, o_ref, *, ctx, group):
    def attend(n_keys):
        k, v = k_ref[0:n_keys, :], v_ref[0:n_keys, :]
        for g in range(group):
            s = _nt_dot(q_ref[:, g * LANES:(g + 1) * LANES], k)
            e = jnp.exp(s - jnp.max(s, axis=-1, keepdims=True))
            inv = 1.0 / jnp.sum(e, axis=-1, keepdims=True)
            o_ref[:, g * LANES:(g + 1) * LANES] = (_dot(e.astype(BF16), v) * inv).astype(BF16)

    is_ctx = pl.program_id(2) == 0
    pl.when(is_ctx)(lambda: attend(ctx))
    pl.when(jnp.logical_not(is_ctx))(lambda: attend(k_ref.shape[0]))


def _attn_a(q, k, v, ctx):
    bsz, t_len, _ = q.shape
    group = A_HEADS // A_KV_HEADS
    qo_spec = pl.BlockSpec((None, TOKEN_BLOCK, group * LANES), lambda b, h, t: (b, t, h))
    kv_spec = pl.BlockSpec((None, t_len, LANES), lambda b, h, t: (b, 0, h))
    return pl.pallas_call(
        functools.partial(_attn_a_kernel, ctx=ctx, group=group),
        grid=(bsz, A_KV_HEADS, t_len // TOKEN_BLOCK),
        in_specs=[qo_spec, kv_spec, kv_spec],
        out_specs=qo_spec,
        out_shape=jax.ShapeDtypeStruct(q.shape, BF16),
        compiler_params=_params(3),
        name="attn_dense_gqa",
    )(q, k, v)


def _attn_b_kernel(sink_ref, q_ref, k_ref, v_ref, o_ref, *, ctx, seq, group):
    h, t = pl.program_id(1), pl.program_id(2)
    lane = lax.broadcasted_iota(jnp.int32, (1, LANES), 1)
    first_head = (lane // (HALF // 2)) % 2 == 0
    span = TOKEN_BLOCK + 2 * WINDOW

    def run(kk, vv, valid):
        for sl in range(group // 2):
            qs = q_ref[:, sl * LANES:(sl + 1) * LANES]
            outs = []
            for j in range(2):
                qm = jnp.where(first_head if j == 0 else jnp.logical_not(first_head), qs, jnp.zeros_like(qs))
                s = _nt_dot(qm, kk)
                if valid is not None:
                    s = jnp.where(valid, s, NEG_INF)
                snk = sink_ref[h * group + sl * 2 + j]
                m = jnp.maximum(jnp.max(s, axis=-1, keepdims=True), snk)
                e = jnp.exp(s - m)
                inv = 1.0 / (jnp.sum(e, axis=-1, keepdims=True) + jnp.exp(snk - m))
                outs.append(_dot(e.astype(BF16), vv) * inv)
            o_ref[:, sl * LANES:(sl + 1) * LANES] = jnp.where(lane < HALF, outs[0], outs[1]).astype(BF16)

    @pl.when(t == 0)
    def _():
        run(k_ref[0:ctx, :], v_ref[0:ctx, :], None)

    @pl.when(t > 0)
    def _():
        p0 = (t - 1) * TOKEN_BLOCK
        start = pl.multiple_of(jnp.clip(p0 - WINDOW, 0, seq - span), WINDOW)
        kk = jnp.concatenate([k_ref[0:ctx, :], k_ref[pl.ds(ctx + start, span), :]], axis=0)
        vv = jnp.concatenate([v_ref[0:ctx, :], v_ref[pl.ds(ctx + start, span), :]], axis=0)
        col = lax.broadcasted_iota(jnp.int32, (TOKEN_BLOCK, ctx + span), 1)
        qpos = p0 + lax.broadcasted_iota(jnp.int32, (TOKEN_BLOCK, ctx + span), 0)
        kpos = start + col - ctx
        valid = jnp.logical_or(col < ctx, jnp.abs(qpos - kpos) <= WINDOW)
        run(kk, vv, valid)


def _attn_b(sink, q, k, v, ctx):
    bsz, t_len, _ = q.shape
    group = B_HEADS // B_KV_HEADS
    width = group // 2 * LANES
    qo_spec = pl.BlockSpec((None, TOKEN_BLOCK, width), lambda b, h, t, s: (b, t, h))
    kv_spec = pl.BlockSpec((None, t_len, LANES), lambda b, h, t, s: (b, 0, h))
    return pl.pallas_call(
        functools.partial(_attn_b_kernel, ctx=ctx, seq=t_len - ctx, group=group),
        grid_spec=pltpu.PrefetchScalarGridSpec(
            num_scalar_prefetch=1,
            grid=(bsz, B_KV_HEADS, t_len // TOKEN_BLOCK),
            in_specs=[qo_spec, kv_spec, kv_spec],
            out_specs=qo_spec,
        ),
        out_shape=jax.ShapeDtypeStruct(q.shape, BF16),
        compiler_params=_params(3),
        name="attn_window_sink",
    )(sink, q, k, v)


def _attn_c_kernel(lam_ref, gain_ref, q_ref, k_ref, v_ref, o_ref, *, ctx, lam_init):
    lane = lax.broadcasted_iota(jnp.int32, (1, LANES), 1)
    first_map = (lane // (HALF // 2)) % 2 == 0
    lam = (jnp.exp(jnp.sum(lam_ref[0:1, :] * lam_ref[1:2, :], axis=-1, keepdims=True))
           - jnp.exp(jnp.sum(lam_ref[2:3, :] * lam_ref[3:4, :], axis=-1, keepdims=True)) + lam_init)

    def attend(n_keys):
        k, v = k_ref[0:n_keys, :], v_ref[0:n_keys, :]
        q = q_ref[...]
        zero = jnp.zeros_like(q)
        s0 = _nt_dot(jnp.where(first_map, q, zero), k)
        s1 = _nt_dot(jnp.where(first_map, zero, q), k)
        e0 = jnp.exp(s0 - jnp.max(s0, axis=-1, keepdims=True))
        e1 = jnp.exp(s1 - jnp.max(s1, axis=-1, keepdims=True))
        r0 = 1.0 / jnp.sum(e0, axis=-1, keepdims=True)
        r1 = lam / jnp.sum(e1, axis=-1, keepdims=True)
        o = _dot((e0 * r0 - e1 * r1).astype(BF16), v)
        y = o * lax.rsqrt(jnp.mean(o * o, axis=-1, keepdims=True) + RMS_EPS) * gain_ref[...]
        o_ref[...] = (y * (1.0 - lam_init)).astype(BF16)

    is_ctx = pl.program_id(2) == 0
    pl.when(is_ctx)(lambda: attend(ctx))
    pl.when(jnp.logical_not(is_ctx))(lambda: attend(k_ref.shape[0]))


def _attn_c(lam_vecs, gain, q, k, v, ctx, lam_init):
    bsz, t_len, _ = q.shape
    qo_spec = pl.BlockSpec((None, TOKEN_BLOCK, LANES), lambda b, h, t: (b, t, h))
    kv_spec = pl.BlockSpec((None, t_len, LANES), lambda b, h, t: (b, 0, h))
    return pl.pallas_call(
        functools.partial(_attn_c_kernel, ctx=ctx, lam_init=lam_init),
        grid=(bsz, C_HEADS, t_len // TOKEN_BLOCK),
        in_specs=[pl.BlockSpec(lam_vecs.shape, lambda b, h, t: (0, 0)),
                  pl.BlockSpec(gain.shape, lambda b, h, t: (0, 0)),
                  qo_spec, kv_spec, kv_spec],
        out_specs=qo_spec,
        out_shape=jax.ShapeDtypeStruct(q.shape, BF16),
        compiler_params=_params(3),
        name="attn_differential",
    )(lam_vecs, gain, q, k, v)


def _layer_norm(z, gain, bias):
    mu = jnp.mean(z, axis=-1, keepdims=True)
    zc = z - mu
    var = jnp.mean(zc * zc, axis=-1, keepdims=True)
    return zc * lax.rsqrt(var + LN_EPS) * gain + bias


def _oproj_router_kernel(o_ref, wo_ref, x_ref, mod_ref, lng_ref, lnb_ref, wr_ref, br_ref,
                         xo_ref, h2_ref, ri_ref, wcol_ref, cnt_ref, carry_ref, *, alpha):
    @pl.when(jnp.logical_and(pl.program_id(0) == 0, pl.program_id(1) == 0))
    def _():
        carry_ref[...] = jnp.zeros_like(carry_ref)

    tb = x_ref.shape[0]
    y = _dot(o_ref[...], wo_ref[...])
    xn = _layer_norm(alpha * x_ref[...] + mod_ref[2:3, :] * y, lng_ref[0:1, :], lnb_ref[0:1, :])
    xo_ref[...] = xn
    h2 = xn * (1.0 + mod_ref[4:5, :]) + mod_ref[3:4, :]
    h2_ref[...] = h2

    h_hi, h_lo = _split_bf16(h2)
    w_hi, w_lo = _split_bf16(wr_ref[...])
    logits = _nt_dot(w_hi, h_hi) + _nt_dot(w_hi, h_lo) + _nt_dot(w_lo, h_hi) + br_ref[...]

    row8 = lax.broadcasted_iota(jnp.int32, (SUBLANES, tb), 0)
    glog = jnp.where(row8 < N_GROUPS, logits[0:SUBLANES, :], NEG_INF)
    gmax = jnp.max(glog, axis=0, keepdims=True)
    gsel = jnp.min(jnp.where(glog == gmax, row8, SUBLANES), axis=0, keepdims=True)
    gprob = 1.0 / jnp.sum(jnp.exp(glog - gmax), axis=0, keepdims=True)
    elog = jnp.zeros((SUBLANES, tb), F32)
    for g in range(N_GROUPS):
        elog = jnp.where(gsel == g, logits[(g + 1) * SUBLANES:(g + 2) * SUBLANES, :], elog)
    v1 = jnp.max(elog, axis=0, keepdims=True)
    i1 = jnp.min(jnp.where(elog == v1, row8, SUBLANES), axis=0, keepdims=True)
    elog2 = jnp.where(row8 == i1, NEG_INF, elog)
    v2 = jnp.max(elog2, axis=0, keepdims=True)
    i2 = jnp.min(jnp.where(elog2 == v2, row8, SUBLANES), axis=0, keepdims=True)
    t21 = jnp.exp(v2 - v1)
    w1 = gprob / (1.0 + t21)
    w2 = gprob * t21 / (1.0 + t21)
    e1 = gsel * EXPERTS_PER_GROUP + i1
    e2 = gsel * EXPERTS_PER_GROUP + i2

    rowe = lax.broadcasted_iota(jnp.int32, (N_EXPERTS, tb), 0)
    oh1, oh2 = rowe == e1, rowe == e2
    oh = jnp.where(jnp.logical_or(oh1, oh2), 1.0, 0.0)
    before = (lax.broadcasted_iota(jnp.int32, (tb, tb), 0) < lax.broadcasted_iota(jnp.int32, (tb, tb), 1))
    base = _dot(oh.astype(BF16), jnp.where(before, 1.0, 0.0).astype(BF16)) + carry_ref[:, 0:1]
    rank1 = jnp.sum(jnp.where(oh1, base, 0.0), axis=0, keepdims=True).astype(jnp.int32)
    rank2 = jnp.sum(jnp.where(oh2, base, 0.0), axis=0, keepdims=True).astype(jnp.int32)
    carry_ref[...] = carry_ref[...] + jnp.sum(oh, axis=1, keepdims=True)
    cnt_ref[...] = carry_ref[...]

    ri_ref[...] = jnp.where(row8 == 0, e1, jnp.where(row8 == 1, e2, jnp.where(row8 == 2, rank1,
                            jnp.where(row8 == 3, rank2, 0))))
    rowl = lax.broadcasted_iota(jnp.int32, (LANES, tb), 0)
    wcol_ref[...] = jnp.where(rowl == 0, w1, jnp.where(rowl == 1, w2, 0.0)).T


def _oproj_router(o, w_o, x, mod, lng, lnb, w_router, b_router, alpha):
    bsz, t_len, d = x.shape
    n_t = t_len // TOKEN_BLOCK
    n_blk = bsz * n_t
    tok = lambda b, t: (b, t, 0)
    blk = lambda b, t: (b * n_t + t, 0, 0)
    const2 = lambda b, t: (0, 0)
    return pl.pallas_call(
        functools.partial(_oproj_router_kernel, alpha=alpha),
        grid=(bsz, n_t),
        in_specs=[
            pl.BlockSpec((None, TOKEN_BLOCK, o.shape[-1]), tok),
            pl.BlockSpec(w_o.shape, const2),
            pl.BlockSpec((None, TOKEN_BLOCK, d), tok),
            pl.BlockSpec((None, 6, d), lambda b, t: (jnp.where(t == 0, bsz, b), 0, 0)),
            pl.BlockSpec(lng.shape, const2),
            pl.BlockSpec(lnb.shape, const2),
            pl.BlockSpec(w_router.shape, const2),
            pl.BlockSpec(b_router.shape, const2),
        ],
        out_specs=[
            pl.BlockSpec((None, TOKEN_BLOCK, d), tok),
            pl.BlockSpec((None, TOKEN_BLOCK, d), tok),
            pl.BlockSpec((None, SUBLANES, TOKEN_BLOCK), blk),
            pl.BlockSpec((None, TOKEN_BLOCK, LANES), blk),
            pl.BlockSpec((N_EXPERTS, LANES), const2),
        ],
        out_shape=[
            jax.ShapeDtypeStruct((bsz, t_len, d), F32),
            jax.ShapeDtypeStruct((bsz, t_len, d), F32),
            jax.ShapeDtypeStruct((n_blk, SUBLANES, TOKEN_BLOCK), jnp.int32),
            jax.ShapeDtypeStruct((n_blk, TOKEN_BLOCK, LANES), F32),
            jax.ShapeDtypeStruct((N_EXPERTS, LANES), F32),
        ],
        scratch_shapes=[pltpu.VMEM((N_EXPERTS, LANES), F32)],
        compiler_params=_params(2),
        name="oproj_ln_router",
    )(o, w_o, x, mod, lng, lnb, w_router, b_router)


def _dispatch_kernel(dest_ref, h_ref, xs_ref, sem):
    tb = h_ref.shape[0]

    def copy(r, k):
        return pltpu.make_async_copy(h_ref.at[pl.ds(r, 1)], xs_ref.at[pl.ds(dest_ref[k, r], 1)], sem)

    def start(r, carry):
        copy(r, 0).start()
        copy(r, 1).start()
        return carry

    def wait(r, carry):
        copy(r, 0).wait()
        copy(r, 1).wait()
        return carry

    lax.fori_loop(0, tb, start, 0)
    lax.fori_loop(0, tb, wait, 0)


def _dispatch(dest, h2, n_rows):
    n_tok, d = h2.shape
    n_blk = n_tok // TOKEN_BLOCK
    return pl.pallas_call(
        _dispatch_kernel,
        grid=(n_blk,),
        in_specs=[
            pl.BlockSpec((None, 2, TOKEN_BLOCK), lambda i: (i, 0, 0), memory_space=pltpu.SMEM),
            pl.BlockSpec((TOKEN_BLOCK, d), lambda i: (i, 0)),
        ],
        out_specs=pl.BlockSpec(memory_space=pl.ANY),
        out_shape=jax.ShapeDtypeStruct((n_rows, d), F32),
        scratch_shapes=[pltpu.SemaphoreType.DMA(())],
        compiler_params=pltpu.CompilerParams(dimension_semantics=("arbitrary",), vmem_limit_bytes=VMEM_LIMIT,
                                             has_side_effects=True),
        name="moe_dispatch",
    )(dest, h2)


def _experts_kernel(te_ref, nu_ref, xs_ref, wg_ref, wu_ref, wd_ref, ys_ref, wg_s, wu_s, wd_s):
    t = pl.program_id(0)

    @pl.when(t < nu_ref[0])
    def _():
        changed = jnp.logical_or(t == 0, te_ref[t] != te_ref[jnp.maximum(t - 1, 0)])

        @pl.when(changed)
        def _():
            wg_s[...] = wg_ref[...].astype(BF16)
            wu_s[...] = wu_ref[...].astype(BF16)
            wd_s[...] = wd_ref[...].astype(BF16)

        x = xs_ref[...].astype(BF16)
        a = _dot(x, wg_s[...])
        u = _dot(x, wu_s[...])
        hid = a * (1.0 / (1.0 + jnp.exp(-a))) * u
        ys_ref[...] = _dot(hid.astype(BF16), wd_s[...])


def _experts(tile_expert, n_used, xs, w_gate, w_up, w_down, layer):
    n_rows, d = xs.shape
    ff = w_gate.shape[-1]
    n_tiles = n_rows // EXPERT_TILE

    def row_map(t, te, nu):
        return (jnp.minimum(t, nu[0] - 1), 0)

    def w_map(t, te, nu):
        e = te[jnp.minimum(t, nu[0] - 1)]
        return (layer, e // EXPERTS_PER_GROUP, e % EXPERTS_PER_GROUP, 0, 0)

    return pl.pallas_call(
        _experts_kernel,
        grid_spec=pltpu.PrefetchScalarGridSpec(
            num_scalar_prefetch=2,
            grid=(n_tiles,),
            in_specs=[
                pl.BlockSpec((EXPERT_TILE, d), row_map),
                pl.BlockSpec((None, None, None, d, ff), w_map),
                pl.BlockSpec((None, None, None, d, ff), w_map),
                pl.BlockSpec((None, None, None, ff, d), w_map),
            ],
            out_specs=pl.BlockSpec((EXPERT_TILE, d), row_map),
            scratch_shapes=[pltpu.VMEM((d, ff), BF16), pltpu.VMEM((d, ff), BF16), pltpu.VMEM((ff, d), BF16)],
        ),
        out_shape=jax.ShapeDtypeStruct((n_rows, d), F32),
        compiler_params=_params(1),
        name="moe_experts",
    )(tile_expert, n_used, xs, w_gate, w_up, w_down)


def _combine_kernel(dest_ref, ys_ref, wcol_ref, x_ref, mod_ref, lng_ref, lnb_ref, o_ref, buf, sem, *, alpha):
    tb = x_ref.shape[0]

    def copy(r, k):
        return pltpu.make_async_copy(ys_ref.at[pl.ds(dest_ref[k, r], 1)], buf.at[k, pl.ds(r, 1)], sem)

    def start(r, carry):
        copy(r, 0).start()
        copy(r, 1).start()
        return carry

    def wait(r, carry):
        copy(r, 0).wait()
        copy(r, 1).wait()
        return carry

    lax.fori_loop(0, tb, start, 0)
    lax.fori_loop(0, tb, wait, 0)
    f = wcol_ref[:, 0:1] * buf[0] + wcol_ref[:, 1:2] * buf[1]
    o_ref[...] = _layer_norm(alpha * x_ref[...] + mod_ref[5:6, :] * f, lng_ref[1:2, :], lnb_ref[1:2, :])


def _combine(dest, ys, wcol, x, mod, lng, lnb, alpha):
    bsz, t_len, d = x.shape
    n_t = t_len // TOKEN_BLOCK
    tok = lambda b, t: (b, t, 0)
    blk = lambda b, t: (b * n_t + t, 0, 0)
    const2 = lambda b, t: (0, 0)
    return pl.pallas_call(
        functools.partial(_combine_kernel, alpha=alpha),
        grid=(bsz, n_t),
        in_specs=[
            pl.BlockSpec((None, 2, TOKEN_BLOCK), blk, memory_space=pltpu.SMEM),
            pl.BlockSpec(memory_space=pl.ANY),
            pl.BlockSpec((None, TOKEN_BLOCK, LANES), blk),
            pl.BlockSpec((None, TOKEN_BLOCK, d), tok),
            pl.BlockSpec((None, 6, d), lambda b, t: (jnp.where(t == 0, bsz, b), 0, 0)),
            pl.BlockSpec(lng.shape, const2),
            pl.BlockSpec(lnb.shape, const2),
        ],
        out_specs=pl.BlockSpec((None, TOKEN_BLOCK, d), tok),
        out_shape=jax.ShapeDtypeStruct(x.shape, F32),
        scratch_shapes=[pltpu.VMEM((2, TOKEN_BLOCK, d), F32), pltpu.SemaphoreType.DMA(())],
        compiler_params=_params(2),
        name="moe_combine_ln",
    )(dest, ys, wcol, x, mod, lng, lnb)


def _qkv_columns(kind):
    if kind == 0:
        perm, *_ = _slot_layout(A_HEAD_DIM // 4)
        nq, nk = A_HEADS, A_KV_HEADS
        q = np.concatenate([s * LANES + perm for s in range(nq)])
        k = np.concatenate([nq * LANES + s * LANES + perm for s in range(nk)])
        v = np.arange((nq + nk) * LANES, (nq + 2 * nk) * LANES)
        return np.concatenate([q, k, v]), nq, nk, nk
    perm, axis_of, freq_of, half_of, _ = _slot_layout(B_HEAD_DIM // 4)
    nf = B_HEAD_DIM // 4
    if kind == 1:
        nq = B_HEADS // 2
        q = np.concatenate([s * LANES + perm for s in range(nq)])
        k_base = B_HEADS * B_HEAD_DIM
        v_base = k_base + B_KV_HEADS * B_HEAD_DIM
        in_head = axis_of * 2 * nf + half_of * nf + freq_of
        k = np.concatenate([k_base + h * B_HEAD_DIM + in_head for h in range(B_KV_HEADS)])
        v = np.concatenate([v_base + h * B_HEAD_DIM + np.arange(LANES) % B_HEAD_DIM for h in range(B_KV_HEADS)])
        return np.concatenate([q, k, v]), nq, B_KV_HEADS, B_KV_HEADS
    nq = C_HEADS
    q = np.concatenate([s * LANES + perm for s in range(nq)])
    k = np.concatenate([nq * LANES + s * LANES + perm for s in range(nq)])
    v = np.arange(2 * nq * LANES, 3 * nq * LANES)
    return np.concatenate([q, k, v]), nq, nq, nq


def kernel(x, c, ctx, c_ctx, w_mod, b_mod, ln_gain, ln_bias, a_w_qkv, a_q_gain, a_k_gain, a_w_o, b_w_qkv, b_sink, b_w_o, c_w_qkv, c_lam_q1, c_lam_k1, c_lam_q2, c_lam_k2, c_subln_gain, c_w_o, moe_w_group, moe_b_group, moe_w_expert, moe_b_expert, moe_w_gate, moe_w_up, moe_w_down):
    bsz, seq, d = x.shape
    n_ctx = ctx.shape[1]
    depth = w_mod.shape[0]
    assert n_ctx == TOKEN_BLOCK and seq % TOKEN_BLOCK == 0 and seq >= TOKEN_BLOCK + 2 * WINDOW
    t_len = n_ctx + seq
    n_tok = bsz * t_len
    alpha = (2.0 * depth) ** 0.25

    mod_rows = -(-(bsz + 1) // SUBLANES) * SUBLANES
    cc = jnp.concatenate([c, c_ctx[None, :], jnp.zeros((mod_rows - bsz - 1, d), F32)], axis=0)
    mod_all = _modulation(cc, w_mod, b_mod).reshape(depth, mod_rows, 6, d)

    xs_rows = 2 * n_tok + N_EXPERTS * EXPERT_TILE
    xs_rows = -(-xs_rows // EXPERT_TILE) * EXPERT_TILE
    n_tiles = xs_rows // EXPERT_TILE

    stream = jnp.concatenate([ctx, x], axis=1)
    tables = {hd: _rope_tables(seq, n_ctx, hd // 4, hd ** -0.5) for hd in (A_HEAD_DIM, B_HEAD_DIM)}
    perm_a = _slot_layout(A_HEAD_DIM // 4)[0]

    for i in range(depth):
        kind, j = i % N_MIXERS, i // N_MIXERS
        mod = mod_all[i]
        cols, n_q, n_k, n_v = _qkv_columns(kind)
        if kind == 0:
            w = a_w_qkv[j][:, cols].astype(BF16)
            gains = (a_q_gain[j][perm_a][None, :], a_k_gain[j][perm_a][None, :])
            q, k, v = _qkv_call(stream, mod, w, tables[A_HEAD_DIM], gains, n_q, n_k, n_v)
            o = _attn_a(q, k, v, n_ctx)
            w_o = a_w_o[j]
        elif kind == 1:
            w = b_w_qkv[j][:, cols].astype(BF16)
            q, k, v = _qkv_call(stream, mod, w, tables[B_HEAD_DIM], None, n_q, n_k, n_v)
            o = _attn_b(b_sink[j], q, k, v, n_ctx)
            w_o = b_w_o[j]
        else:
            w = c_w_qkv[j][:, cols].astype(BF16)
            q, k, v = _qkv_call(stream, mod, w, tables[C_HEAD_DIM], None, n_q, n_k, n_v)
            lam_init = 0.8 - 0.6 * float(np.exp(-0.3 * i))
            lam_vecs = jnp.stack([c_lam_q1[j], c_lam_k1[j], c_lam_q2[j], c_lam_k2[j]], axis=0)
            o = _attn_c(lam_vecs, c_subln_gain[j][None, :], q, k, v, n_ctx, lam_init)
            w_o = c_w_o[j]

        w_router = jnp.zeros((ROUTER_ROWS, d), F32)
        w_router = w_router.at[0:N_GROUPS].set(moe_w_group[i].T).at[SUBLANES:SUBLANES + N_EXPERTS].set(moe_w_expert[i].T)
        b_router = jnp.zeros((ROUTER_ROWS, 1), F32)
        b_router = b_router.at[0:N_GROUPS, 0].set(moe_b_group[i]).at[SUBLANES:SUBLANES + N_EXPERTS, 0].set(moe_b_expert[i])

        stream, h2, rinfo, wcol, counts = _oproj_router(
            o, w_o.astype(BF16), stream, mod, ln_gain[i], ln_bias[i], w_router, b_router, alpha)

        cnt = counts[:, 0].astype(jnp.int32)
        tiles = (cnt + EXPERT_TILE - 1) // EXPERT_TILE
        tile_end = jnp.cumsum(tiles)
        row_off = (tile_end - tiles) * EXPERT_TILE
        n_used = tile_end[-1:].astype(jnp.int32)
        tile_expert = jnp.minimum(
            jnp.searchsorted(tile_end, jnp.arange(n_tiles, dtype=jnp.int32), side="right"), N_EXPERTS - 1
        ).astype(jnp.int32)
        dest = row_off[rinfo[:, 0:2, :]] + rinfo[:, 2:4, :]

        xs = _dispatch(dest, h2.reshape(n_tok, d), xs_rows)
        ys = _experts(tile_expert, n_used, xs, moe_w_gate, moe_w_up, moe_w_down, i)
        stream = _combine(dest, ys, wcol.reshape(-1, TOKEN_BLOCK, LANES), stream, mod, ln_gain[i], ln_bias[i], alpha)

    return stream[:, n_ctx:, :]
```

```python
import functools

import numpy as np
import jax
import jax.numpy as jnp
from jax import lax
from jax.experimental import pallas as pl
from jax.experimental.pallas import tpu as pltpu

GRID_W = 64
ROPE_THETA = 10000.0
LN_EPS = 1e-5
RMS_EPS = 1e-6
NEG_INF = -1e30
N_MIXERS = 3

A_HEADS, A_KV_HEADS, A_HEAD_DIM = 8, 2, 128
B_HEADS, B_KV_HEADS, B_HEAD_DIM = 16, 4, 64
WINDOW = 128
C_HEADS, C_HEAD_DIM = 8, 64
N_GROUPS, EXPERTS_PER_GROUP = 4, 8
N_EXPERTS = N_GROUPS * EXPERTS_PER_GROUP

LANES = 128
SUBLANES = 8
HALF = LANES // 2
TOKEN_BLOCK = 256
EXPERT_TILE = 256
ROUTER_ROWS = 128
ROW_UNROLL = 8
VMEM_LIMIT = 56 * 1024 * 1024

F32 = jnp.float32
BF16 = jnp.bfloat16


def _nt_dot(a, b):
    return lax.dot_general(a, b, (((1,), (1,)), ((), ())), preferred_element_type=F32)


def _dot(a, b):
    return jnp.dot(a, b, preferred_element_type=F32)


def _split_bf16(x):
    hi = x.astype(BF16)
    lo = (x - hi.astype(F32)).astype(BF16)
    return hi, lo


def _params(n_axes):
    return pltpu.CompilerParams(dimension_semantics=("arbitrary",) * n_axes, vmem_limit_bytes=VMEM_LIMIT)


def _slot_layout(n_freq):
    n_sub = LANES // (4 * n_freq)
    perm = np.zeros(LANES, np.int32)
    axis_of = np.zeros(LANES, np.int32)
    freq_of = np.zeros(LANES, np.int32)
    half_of = np.zeros(LANES, np.int32)
    sub_of = np.zeros(LANES, np.int32)
    for half in range(2):
        for sub in range(n_sub):
            for axis in range(2):
                for f in range(n_freq):
                    l = half * HALF + sub * 2 * n_freq + axis * n_freq + f
                    perm[l] = sub * 4 * n_freq + axis * 2 * n_freq + half * n_freq + f
                    axis_of[l], freq_of[l], half_of[l], sub_of[l] = axis, f, half, sub
    return perm, axis_of, freq_of, half_of, sub_of


def _rope_tables(seq, ctx, n_freq, q_scale):
    _, axis_of, freq_of, half_of, _ = _slot_layout(n_freq)
    rows = seq // GRID_W
    row = jnp.broadcast_to(jnp.arange(rows, dtype=F32)[:, None], (rows, GRID_W)).reshape(-1)
    col = jnp.broadcast_to(jnp.arange(GRID_W, dtype=F32)[None, :], (rows, GRID_W)).reshape(-1)
    inv = ROPE_THETA ** (-jnp.arange(n_freq, dtype=F32) / n_freq)
    ang = jnp.stack([row[:, None] * inv, col[:, None] * inv], axis=1)
    cos, sin = jnp.cos(ang), jnp.sin(ang)
    cos_l = cos[:, axis_of, freq_of]
    sin_l = sin[:, axis_of, freq_of] * jnp.asarray(np.where(half_of == 0, -1.0, 1.0), F32)
    cos_l = jnp.concatenate([jnp.ones((ctx, LANES), F32), cos_l], axis=0)
    sin_l = jnp.concatenate([jnp.zeros((ctx, LANES), F32), sin_l], axis=0)
    return cos_l * q_scale, sin_l * q_scale, cos_l, sin_l


def _mod_kernel(c_ref, w_ref, b_ref, o_ref):
    c = c_ref[...]
    s = c * (1.0 / (1.0 + jnp.exp(-c)))
    s_hi, s_lo = _split_bf16(s)
    w_hi, w_lo = _split_bf16(w_ref[...])
    o_ref[...] = _dot(s_hi, w_hi) + _dot(s_lo, w_hi) + _dot(s_hi, w_lo) + b_ref[...]


def _modulation(cc, w_mod, b_mod):
    depth, d, n = w_mod.shape
    rows = cc.shape[0]
    tn = n // 4
    return pl.pallas_call(
        _mod_kernel,
        grid=(depth, n // tn),
        in_specs=[
            pl.BlockSpec((rows, d), lambda l, j: (0, 0)),
            pl.BlockSpec((None, d, tn), lambda l, j: (l, 0, j)),
            pl.BlockSpec((None, 1, tn), lambda l, j: (l, 0, j)),
        ],
        out_specs=pl.BlockSpec((None, rows, tn), lambda l, j: (l, 0, j)),
        out_shape=jax.ShapeDtypeStruct((depth, rows, n), F32),
        compiler_params=_params(2),
        name="modulation",
    )(cc, w_mod, b_mod.reshape(depth, 1, n))


def _qkv_kernel(*refs, n_q, n_k, rms):
    if rms:
        x_ref, mod_ref, w_ref, cq_ref, sq_ref, ck_ref, sk_ref, gq_ref, gk_ref, q_ref, k_ref, v_ref = refs
    else:
        x_ref, mod_ref, w_ref, cq_ref, sq_ref, ck_ref, sk_ref, q_ref, k_ref, v_ref = refs
        gq_ref = gk_ref = None
    sh, sc = mod_ref[0:1, :], mod_ref[1:2, :]
    h = (x_ref[...] * (1.0 + sc) + sh).astype(BF16)

    def rope_slots(y, cos, sin, gain_ref, out_ref, n_slots):
        for j in range(n_slots):
            yj = y[:, j * LANES:(j + 1) * LANES]
            if gain_ref is not None:
                ms = jnp.mean(yj * yj, axis=-1, keepdims=True)
                yj = yj * lax.rsqrt(ms + RMS_EPS) * gain_ref[...]
            out = yj * cos + pltpu.roll(yj, HALF, 1) * sin
            out_ref[:, j * LANES:(j + 1) * LANES] = out.astype(BF16)

    nq, nk = n_q * LANES, n_k * LANES
    rope_slots(_dot(h, w_ref[:, 0:nq]), cq_ref[...], sq_ref[...], gq_ref, q_ref, n_q)
    rope_slots(_dot(h, w_ref[:, nq:nq + nk]), ck_ref[...], sk_ref[...], gk_ref, k_ref, n_k)
    v_ref[...] = _dot(h, w_ref[:, nq + nk:]).astype(BF16)


def _qkv_call(x, mod, w, tables, gains, n_q, n_k, n_v):
    bsz, t_len, d = x.shape
    n_t = t_len // TOKEN_BLOCK
    n_rows = mod.shape[0]
    rms = gains is not None
    tok = lambda b, t: (b, t, 0)
    tab_spec = pl.BlockSpec((TOKEN_BLOCK, LANES), lambda b, t: (t, 0))
    in_specs = [
        pl.BlockSpec((None, TOKEN_BLOCK, d), tok),
        pl.BlockSpec((None, 6, d), lambda b, t: (jnp.where(t == 0, bsz, b), 0, 0)),
        pl.BlockSpec(w.shape, lambda b, t: (0, 0)),
        tab_spec, tab_spec, tab_spec, tab_spec,
    ]
    args = [x, mod, w, *tables]
    if rms:
        in_specs += [pl.BlockSpec((1, LANES), lambda b, t: (0, 0))] * 2
        args += list(gains)
    del n_rows
    return pl.pallas_call(
        functools.partial(_qkv_kernel, n_q=n_q, n_k=n_k, rms=rms),
        grid=(bsz, n_t),
        in_specs=in_specs,
        out_specs=[pl.BlockSpec((None, TOKEN_BLOCK, n * LANES), tok) for n in (n_q, n_k, n_v)],
        out_shape=[jax.ShapeDtypeStruct((bsz, t_len, n * LANES), BF16) for n in (n_q, n_k, n_v)],
        compiler_params=_params(2),
        name="qkv_rope",
    )(*args)


def _attn_a_kernel(q_ref, k_ref, v_ref, o_ref, *, ctx, group):
    def attend(n_keys):
        k, v = k_ref[0:n_keys, :], v_ref[0:n_keys, :]
        for g in range(group):
            s = _nt_dot(q_ref[:, g * LANES:(g + 1) * LANES], k)
            e = jnp.exp(s - jnp.max(s, axis=-1, keepdims=True))
            inv = 1.0 / jnp.sum(e, axis=-1, keepdims=True)
            o_ref[:, g * LANES:(g + 1) * LANES] = (_dot(e.astype(BF16), v) * inv).astype(BF16)

    is_ctx = pl.program_id(2) == 0
    pl.when(is_ctx)(lambda: attend(ctx))
    pl.when(jnp.logical_not(is_ctx))(lambda: attend(k_ref.shape[0]))


def _attn_a(q, k, v, ctx):
    bsz, t_len, _ = q.shape
    group = A_HEADS // A_KV_HEADS
    qo_spec = pl.BlockSpec((None, TOKEN_BLOCK, group * LANES), lambda b, h, t: (b, t, h))
    kv_spec = pl.BlockSpec((None, t_len, LANES), lambda b, h, t: (b, 0, h))
    return pl.pallas_call(
        functools.partial(_attn_a_kernel, ctx=ctx, group=group),
        grid=(bsz, A_KV_HEADS, t_len // TOKEN_BLOCK),
        in_specs=[qo_spec, kv_spec, kv_spec],
        out_specs=qo_spec,
        out_shape=jax.ShapeDtypeStruct(q.shape, BF16),
        compiler_params=_params(3),
        name="attn_dense_gqa",
    )(q, k, v)


def _attn_b_kernel(sink_ref, q_ref, k_ref, v_ref, o_ref, *, ctx, seq, group):
    h, t = pl.program_id(1), pl.program_id(2)
    lane = lax.broadcasted_iota(jnp.int32, (1, LANES), 1)
    first_head = (lane // (HALF // 2)) % 2 == 0
    span = TOKEN_BLOCK + 2 * WINDOW

    def run(kk, vv, valid):
        for sl in range(group // 2):
            qs = q_ref[:, sl * LANES:(sl + 1) * LANES]
            outs = []
            for j in range(2):
                qm = jnp.where(first_head if j == 0 else jnp.logical_not(first_head), qs, jnp.zeros_like(qs))
                s = _nt_dot(qm, kk)
                if valid is not None:
                    s = jnp.where(valid, s, NEG_INF)
                snk = sink_ref[h * group + sl * 2 + j]
                m = jnp.maximum(jnp.max(s, axis=-1, keepdims=True), snk)
                e = jnp.exp(s - m)
                inv = 1.0 / (jnp.sum(e, axis=-1, keepdims=True) + jnp.exp(snk - m))
                outs.append(_dot(e.astype(BF16), vv) * inv)
            o_ref[:, sl * LANES:(sl + 1) * LANES] = jnp.where(lane < HALF, outs[0], outs[1]).astype(BF16)

    @pl.when(t == 0)
    def _():
        run(k_ref[0:ctx, :], v_ref[0:ctx, :], None)

    @pl.when(t > 0)
    def _():
        p0 = (t - 1) * TOKEN_BLOCK
        start = pl.multiple_of(jnp.clip(p0 - WINDOW, 0, seq - span), WINDOW)
        kk = jnp.concatenate([k_ref[0:ctx, :], k_ref[pl.ds(ctx + start, span), :]], axis=0)
        vv = jnp.concatenate([v_ref[0:ctx, :], v_ref[pl.ds(ctx + start, span), :]], axis=0)
        col = lax.broadcasted_iota(jnp.int32, (TOKEN_BLOCK, ctx + span), 1)
        qpos = p0 + lax.broadcasted_iota(jnp.int32, (TOKEN_BLOCK, ctx + span), 0)
        kpos = start + col - ctx
        valid = jnp.logical_or(col < ctx, jnp.abs(qpos - kpos) <= WINDOW)
        run(kk, vv, valid)


def _attn_b(sink, q, k, v, ctx):
    bsz, t_len, _ = q.shape
    group = B_HEADS // B_KV_HEADS
    width = group // 2 * LANES
    qo_spec = pl.BlockSpec((None, TOKEN_BLOCK, width), lambda b, h, t, s: (b, t, h))
    kv_spec = pl.BlockSpec((None, t_len, LANES), lambda b, h, t, s: (b, 0, h))
    return pl.pallas_call(
        functools.partial(_attn_b_kernel, ctx=ctx, seq=t_len - ctx, group=group),
        grid_spec=pltpu.PrefetchScalarGridSpec(
            num_scalar_prefetch=1,
            grid=(bsz, B_KV_HEADS, t_len // TOKEN_BLOCK),
            in_specs=[qo_spec, kv_spec, kv_spec],
            out_specs=qo_spec,
        ),
        out_shape=jax.ShapeDtypeStruct(q.shape, BF16),
        compiler_params=_params(3),
        name="attn_window_sink",
    )(sink, q, k, v)


def _attn_c_kernel(lam_ref, gain_ref, q_ref, k_ref, v_ref, o_ref, *, ctx, lam_init):
    lane = lax.broadcasted_iota(jnp.int32, (1, LANES), 1)
    first_map = (lane // (HALF // 2)) % 2 == 0
    lam = (jnp.exp(jnp.sum(lam_ref[0:1, :] * lam_ref[1:2, :], axis=-1, keepdims=True))
           - jnp.exp(jnp.sum(lam_ref[2:3, :] * lam_ref[3:4, :], axis=-1, keepdims=True)) + lam_init)

    def attend(n_keys):
        k, v = k_ref[0:n_keys, :], v_ref[0:n_keys, :]
        q = q_ref[...]
        zero = jnp.zeros_like(q)
        s0 = _nt_dot(jnp.where(first_map, q, zero), k)
        s1 = _nt_dot(jnp.where(first_map, zero, q), k)
        e0 = jnp.exp(s0 - jnp.max(s0, axis=-1, keepdims=True))
        e1 = jnp.exp(s1 - jnp.max(s1, axis=-1, keepdims=True))
        r0 = 1.0 / jnp.sum(e0, axis=-1, keepdims=True)
        r1 = lam / jnp.sum(e1, axis=-1, keepdims=True)
        o = _dot((e0 * r0 - e1 * r1).astype(BF16), v)
        y = o * lax.rsqrt(jnp.mean(o * o, axis=-1, keepdims=True) + RMS_EPS) * gain_ref[...]
        o_ref[...] = (y * (1.0 - lam_init)).astype(BF16)

    is_ctx = pl.program_id(2) == 0
    pl.when(is_ctx)(lambda: attend(ctx))
    pl.when(jnp.logical_not(is_ctx))(lambda: attend(k_ref.shape[0]))


def _attn_c(lam_vecs, gain, q, k, v, ctx, lam_init):
    bsz, t_len, _ = q.shape
    qo_spec = pl.BlockSpec((None, TOKEN_BLOCK, LANES), lambda b, h, t: (b, t, h))
    kv_spec = pl.BlockSpec((None, t_len, LANES), lambda b, h, t: (b, 0, h))
    return pl.pallas_call(
        functools.partial(_attn_c_kernel, ctx=ctx, lam_init=lam_init),
        grid=(bsz, C_HEADS, t_len // TOKEN_BLOCK),
        in_specs=[pl.BlockSpec(lam_vecs.shape, lambda b, h, t: (0, 0)),
                  pl.BlockSpec(gain.shape, lambda b, h, t: (0, 0)),
                  qo_spec, kv_spec, kv_spec],
        out_specs=qo_spec,
        out_shape=jax.ShapeDtypeStruct(q.shape, BF16),
        compiler_params=_params(3),
        name="attn_differential",
    )(lam_vecs, gain, q, k, v)


def _layer_norm(z, gain, bias):
    mu = jnp.mean(z, axis=-1, keepdims=True)
    zc = z - mu
    var = jnp.mean(zc * zc, axis=-1, keepdims=True)
    return zc * lax.rsqrt(var + LN_EPS) * gain + bias


def _oproj_router_kernel(o_ref, wo_ref, x_ref, mod_ref, lng_ref, lnb_ref, wr_ref, br_ref,
                         xo_ref, h2_ref, ri_ref, wcol_ref, cnt_ref, carry_ref, *, alpha):
    @pl.when(jnp.logical_and(pl.program_id(0) == 0, pl.program_id(1) == 0))
    def _():
        carry_ref[...] = jnp.zeros_like(carry_ref)

    tb = x_ref.shape[0]
    y = _dot(o_ref[...], wo_ref[...])
    xn = _layer_norm(alpha * x_ref[...] + mod_ref[2:3, :] * y, lng_ref[0:1, :], lnb_ref[0:1, :])
    xo_ref[...] = xn
    h2 = xn * (1.0 + mod_ref[4:5, :]) + mod_ref[3:4, :]
    h2_ref[...] = h2

    h_hi, h_lo = _split_bf16(h2)
    w_hi, w_lo = _split_bf16(wr_ref[...])
    logits = _nt_dot(w_hi, h_hi) + _nt_dot(w_hi, h_lo) + _nt_dot(w_lo, h_hi) + br_ref[...]

    row8 = lax.broadcasted_iota(jnp.int32, (SUBLANES, tb), 0)
    glog = jnp.where(row8 < N_GROUPS, logits[0:SUBLANES, :], NEG_INF)
    gmax = jnp.max(glog, axis=0, keepdims=True)
    gsel = jnp.min(jnp.where(glog == gmax, row8, SUBLANES), axis=0, keepdims=True)
    gprob = 1.0 / jnp.sum(jnp.exp(glog - gmax), axis=0, keepdims=True)
    elog = jnp.zeros((SUBLANES, tb), F32)
    for g in range(N_GROUPS):
        elog = jnp.where(gsel == g, logits[(g + 1) * SUBLANES:(g + 2) * SUBLANES, :], elog)
    v1 = jnp.max(elog, axis=0, keepdims=True)
    i1 = jnp.min(jnp.where(elog == v1, row8, SUBLANES), axis=0, keepdims=True)
    elog2 = jnp.where(row8 == i1, NEG_INF, elog)
    v2 = jnp.max(elog2, axis=0, keepdims=True)
    i2 = jnp.min(jnp.where(elog2 == v2, row8, SUBLANES), axis=0, keepdims=True)
    t21 = jnp.exp(v2 - v1)
    w1 = gprob / (1.0 + t21)
    w2 = gprob * t21 / (1.0 + t21)
    e1 = gsel * EXPERTS_PER_GROUP + i1
    e2 = gsel * EXPERTS_PER_GROUP + i2

    rowe = lax.broadcasted_iota(jnp.int32, (N_EXPERTS, tb), 0)
    oh1, oh2 = rowe == e1, rowe == e2
    oh = jnp.where(jnp.logical_or(oh1, oh2), 1.0, 0.0)
    before = (lax.broadcasted_iota(jnp.int32, (tb, tb), 0) < lax.broadcasted_iota(jnp.int32, (tb, tb), 1))
    base = _dot(oh.astype(BF16), jnp.where(before, 1.0, 0.0).astype(BF16)) + carry_ref[:, 0:1]
    rank1 = jnp.sum(jnp.where(oh1, base, 0.0), axis=0, keepdims=True).astype(jnp.int32)
    rank2 = jnp.sum(jnp.where(oh2, base, 0.0), axis=0, keepdims=True).astype(jnp.int32)
    carry_ref[...] = carry_ref[...] + jnp.sum(oh, axis=1, keepdims=True)
    cnt_ref[...] = carry_ref[...]

    ri_ref[...] = jnp.where(row8 == 0, e1, jnp.where(row8 == 1, e2, jnp.where(row8 == 2, rank1,
                            jnp.where(row8 == 3, rank2, 0))))
    rowl = lax.broadcasted_iota(jnp.int32, (LANES, tb), 0)
    wcol_ref[...] = jnp.where(rowl == 0, w1, jnp.where(rowl == 1, w2, 0.0)).T


def _oproj_router(o, w_o, x, mod, lng, lnb, w_router, b_router, alpha):
    bsz, t_len, d = x.shape
    n_t = t_len // TOKEN_BLOCK
    n_blk = bsz * n_t
    tok = lambda b, t: (b, t, 0)
    blk = lambda b, t: (b * n_t + t, 0, 0)
    const2 = lambda b, t: (0, 0)
    return pl.pallas_call(
        functools.partial(_oproj_router_kernel, alpha=alpha),
        grid=(bsz, n_t),
        in_specs=[
            pl.BlockSpec((None, TOKEN_BLOCK, o.shape[-1]), tok),
            pl.BlockSpec(w_o.shape, const2),
            pl.BlockSpec((None, TOKEN_BLOCK, d), tok),
            pl.BlockSpec((None, 6, d), lambda b, t: (jnp.where(t == 0, bsz, b), 0, 0)),
            pl.BlockSpec(lng.shape, const2),
            pl.BlockSpec(lnb.shape, const2),
            pl.BlockSpec(w_router.shape, const2),
            pl.BlockSpec(b_router.shape, const2),
        ],
        out_specs=[
            pl.BlockSpec((None, TOKEN_BLOCK, d), tok),
            pl.BlockSpec((None, TOKEN_BLOCK, d), tok),
            pl.BlockSpec((None, SUBLANES, TOKEN_BLOCK), blk),
            pl.BlockSpec((None, TOKEN_BLOCK, LANES), blk),
            pl.BlockSpec((N_EXPERTS, LANES), const2),
        ],
        out_shape=[
            jax.ShapeDtypeStruct((bsz, t_len, d), F32),
            jax.ShapeDtypeStruct((bsz, t_len, d), F32),
            jax.ShapeDtypeStruct((n_blk, SUBLANES, TOKEN_BLOCK), jnp.int32),
            jax.ShapeDtypeStruct((n_blk, TOKEN_BLOCK, LANES), F32),
            jax.ShapeDtypeStruct((N_EXPERTS, LANES), F32),
        ],
        scratch_shapes=[pltpu.VMEM((N_EXPERTS, LANES), F32)],
        compiler_params=_params(2),
        name="oproj_ln_router",
    )(o, w_o, x, mod, lng, lnb, w_router, b_router)


def _for_each_row(n_rows, fn):
    def chunk(c, carry):
        for u in range(ROW_UNROLL):
            fn(c * ROW_UNROLL + u)
        return carry

    lax.fori_loop(0, n_rows // ROW_UNROLL, chunk, 0)


def _expert_row(off_ref, ri_ref, k, r):
    return off_ref[ri_ref[k, r]] + ri_ref[2 + k, r]


def _dispatch_kernel(off_ref, ri_ref, h_ref, xs_ref, sem):
    tb = h_ref.shape[0]

    def start(r):
        for k in range(2):
            pltpu.make_async_copy(h_ref.at[pl.ds(r, 1)], xs_ref.at[pl.ds(_expert_row(off_ref, ri_ref, k, r), 1)],
                                  sem).start()

    _for_each_row(tb, start)
    for k in range(2):
        pltpu.make_async_copy(h_ref, xs_ref.at[pl.ds(0, tb)], sem).wait()


def _dispatch(row_off, rinfo, h2, n_rows):
    n_tok, d = h2.shape
    n_blk = n_tok // TOKEN_BLOCK
    return pl.pallas_call(
        _dispatch_kernel,
        grid_spec=pltpu.PrefetchScalarGridSpec(
            num_scalar_prefetch=1,
            grid=(n_blk,),
            in_specs=[
                pl.BlockSpec((None, SUBLANES, TOKEN_BLOCK), lambda i, off: (i, 0, 0), memory_space=pltpu.SMEM),
                pl.BlockSpec((TOKEN_BLOCK, d), lambda i, off: (i, 0)),
            ],
            out_specs=pl.BlockSpec(memory_space=pl.ANY),
            scratch_shapes=[pltpu.SemaphoreType.DMA(())],
        ),
        out_shape=jax.ShapeDtypeStruct((n_rows, d), F32),
        compiler_params=_params(1),
        name="moe_dispatch",
    )(row_off, rinfo, h2)


def _experts_kernel(te_ref, nu_ref, xs_ref, wg_ref, wu_ref, wd_ref, ys_ref, wg_s, wu_s, wd_s):
    t = pl.program_id(0)

    @pl.when(t < nu_ref[0])
    def _():
        changed = jnp.logical_or(t == 0, te_ref[t] != te_ref[jnp.maximum(t - 1, 0)])

        @pl.when(changed)
        def _():
            wg_s[...] = wg_ref[...].astype(BF16)
            wu_s[...] = wu_ref[...].astype(BF16)
            wd_s[...] = wd_ref[...].astype(BF16)

        x = xs_ref[...].astype(BF16)
        a = _dot(x, wg_s[...])
        u = _dot(x, wu_s[...])
        hid = a * (1.0 / (1.0 + jnp.exp(-a))) * u
        ys_ref[...] = _dot(hid.astype(BF16), wd_s[...])


def _experts(tile_expert, n_used, xs, w_gate, w_up, w_down, layer):
    n_rows, d = xs.shape
    ff = w_gate.shape[-1]
    n_tiles = n_rows // EXPERT_TILE

    def row_map(t, te, nu):
        return (jnp.minimum(t, nu[0] - 1), 0)

    def w_map(t, te, nu):
        e = te[jnp.minimum(t, nu[0] - 1)]
        return (layer, e // EXPERTS_PER_GROUP, e % EXPERTS_PER_GROUP, 0, 0)

    return pl.pallas_call(
        _experts_kernel,
        grid_spec=pltpu.PrefetchScalarGridSpec(
            num_scalar_prefetch=2,
            grid=(n_tiles,),
            in_specs=[
                pl.BlockSpec((EXPERT_TILE, d), row_map),
                pl.BlockSpec((None, None, None, d, ff), w_map),
                pl.BlockSpec((None, None, None, d, ff), w_map),
                pl.BlockSpec((None, None, None, ff, d), w_map),
            ],
            out_specs=pl.BlockSpec((EXPERT_TILE, d), row_map),
            scratch_shapes=[pltpu.VMEM((d, ff), BF16), pltpu.VMEM((d, ff), BF16), pltpu.VMEM((ff, d), BF16)],
        ),
        out_shape=jax.ShapeDtypeStruct((n_rows, d), F32),
        compiler_params=_params(1),
        name="moe_experts",
    )(tile_expert, n_used, xs, w_gate, w_up, w_down)


def _combine_kernel(off_ref, ri_ref, ys_ref, wcol_ref, x_ref, mod_ref, lng_ref, lnb_ref, o_ref, buf, sem, *, alpha):
    tb = x_ref.shape[0]

    def start(r):
        for k in range(2):
            pltpu.make_async_copy(ys_ref.at[pl.ds(_expert_row(off_ref, ri_ref, k, r), 1)], buf.at[k, pl.ds(r, 1)],
                                  sem).start()

    _for_each_row(tb, start)
    for k in range(2):
        pltpu.make_async_copy(ys_ref.at[pl.ds(0, tb)], buf.at[k], sem).wait()
    f = wcol_ref[:, 0:1] * buf[0] + wcol_ref[:, 1:2] * buf[1]
    o_ref[...] = _layer_norm(alpha * x_ref[...] + mod_ref[5:6, :] * f, lng_ref[1:2, :], lnb_ref[1:2, :])


def _combine(row_off, rinfo, ys, wcol, x, mod, lng, lnb, alpha):
    bsz, t_len, d = x.shape
    n_t = t_len // TOKEN_BLOCK
    tok = lambda b, t, off: (b, t, 0)
    blk = lambda b, t, off: (b * n_t + t, 0, 0)
    const2 = lambda b, t, off: (0, 0)
    return pl.pallas_call(
        functools.partial(_combine_kernel, alpha=alpha),
        grid_spec=pltpu.PrefetchScalarGridSpec(
            num_scalar_prefetch=1,
            grid=(bsz, n_t),
            in_specs=[
                pl.BlockSpec((None, SUBLANES, TOKEN_BLOCK), blk, memory_space=pltpu.SMEM),
                pl.BlockSpec(memory_space=pl.ANY),
                pl.BlockSpec((None, TOKEN_BLOCK, LANES), blk),
                pl.BlockSpec((None, TOKEN_BLOCK, d), tok),
                pl.BlockSpec((None, 6, d), lambda b, t, off: (jnp.where(t == 0, bsz, b), 0, 0)),
                pl.BlockSpec(lng.shape, const2),
                pl.BlockSpec(lnb.shape, const2),
            ],
            out_specs=pl.BlockSpec((None, TOKEN_BLOCK, d), tok),
            scratch_shapes=[pltpu.VMEM((2, TOKEN_BLOCK, d), F32), pltpu.SemaphoreType.DMA(())],
        ),
        out_shape=jax.ShapeDtypeStruct(x.shape, F32),
        compiler_params=_params(2),
        name="moe_combine_ln",
    )(row_off, rinfo, ys, wcol, x, mod, lng, lnb)


def _qkv_columns(kind):
    if kind == 0:
        perm, *_ = _slot_layout(A_HEAD_DIM // 4)
        nq, nk = A_HEADS, A_KV_HEADS
        q = np.concatenate([s * LANES + perm for s in range(nq)])
        k = np.concatenate([nq * LANES + s * LANES + perm for s in range(nk)])
        v = np.arange((nq + nk) * LANES, (nq + 2 * nk) * LANES)
        return np.concatenate([q, k, v]), nq, nk, nk
    perm, axis_of, freq_of, half_of, _ = _slot_layout(B_HEAD_DIM // 4)
    nf = B_HEAD_DIM // 4
    if kind == 1:
        nq = B_HEADS // 2
        q = np.concatenate([s * LANES + perm for s in range(nq)])
        k_base = B_HEADS * B_HEAD_DIM
        v_base = k_base + B_KV_HEADS * B_HEAD_DIM
        in_head = axis_of * 2 * nf + half_of * nf + freq_of
        k = np.concatenate([k_base + h * B_HEAD_DIM + in_head for h in range(B_KV_HEADS)])
        v = np.concatenate([v_base + h * B_HEAD_DIM + np.arange(LANES) % B_HEAD_DIM for h in range(B_KV_HEADS)])
        return np.concatenate([q, k, v]), nq, B_KV_HEADS, B_KV_HEADS
    nq = C_HEADS
    q = np.concatenate([s * LANES + perm for s in range(nq)])
    k = np.concatenate([nq * LANES + s * LANES + perm for s in range(nq)])
    v = np.arange(2 * nq * LANES, 3 * nq * LANES)
    return np.concatenate([q, k, v]), nq, nq, nq


def kernel(x, c, ctx, c_ctx, w_mod, b_mod, ln_gain, ln_bias, a_w_qkv, a_q_gain, a_k_gain, a_w_o, b_w_qkv, b_sink, b_w_o, c_w_qkv, c_lam_q1, c_lam_k1, c_lam_q2, c_lam_k2, c_subln_gain, c_w_o, moe_w_group, moe_b_group, moe_w_expert, moe_b_expert, moe_w_gate, moe_w_up, moe_w_down):
    bsz, seq, d = x.shape
    n_ctx = ctx.shape[1]
    depth = w_mod.shape[0]
    assert n_ctx == TOKEN_BLOCK and seq % TOKEN_BLOCK == 0 and seq >= TOKEN_BLOCK + 2 * WINDOW
    t_len = n_ctx + seq
    n_tok = bsz * t_len
    alpha = (2.0 * depth) ** 0.25

    mod_rows = -(-(bsz + 1) // SUBLANES) * SUBLANES
    cc = jnp.concatenate([c, c_ctx[None, :], jnp.zeros((mod_rows - bsz - 1, d), F32)], axis=0)
    mod_all = _modulation(cc, w_mod, b_mod).reshape(depth, mod_rows, 6, d)

    xs_rows = 2 * n_tok + N_EXPERTS * EXPERT_TILE
    xs_rows = -(-xs_rows // EXPERT_TILE) * EXPERT_TILE
    n_tiles = xs_rows // EXPERT_TILE

    stream = jnp.concatenate([ctx, x], axis=1)
    tables = {hd: _rope_tables(seq, n_ctx, hd // 4, hd ** -0.5) for hd in (A_HEAD_DIM, B_HEAD_DIM)}
    perm_a = _slot_layout(A_HEAD_DIM // 4)[0]

    for i in range(depth):
        kind, j = i % N_MIXERS, i // N_MIXERS
        mod = mod_all[i]
        cols, n_q, n_k, n_v = _qkv_columns(kind)
        if kind == 0:
            w = a_w_qkv[j][:, cols].astype(BF16)
            gains = (a_q_gain[j][perm_a][None, :], a_k_gain[j][perm_a][None, :])
            q, k, v = _qkv_call(stream, mod, w, tables[A_HEAD_DIM], gains, n_q, n_k, n_v)
            o = _attn_a(q, k, v, n_ctx)
            w_o = a_w_o[j]
        elif kind == 1:
            w = b_w_qkv[j][:, cols].astype(BF16)
            q, k, v = _qkv_call(stream, mod, w, tables[B_HEAD_DIM], None, n_q, n_k, n_v)
            o = _attn_b(b_sink[j], q, k, v, n_ctx)
            w_o = b_w_o[j]
        else:
            w = c_w_qkv[j][:, cols].astype(BF16)
            q, k, v = _qkv_call(stream, mod, w, tables[C_HEAD_DIM], None, n_q, n_k, n_v)
            lam_init = 0.8 - 0.6 * float(np.exp(-0.3 * i))
            lam_vecs = jnp.stack([c_lam_q1[j], c_lam_k1[j], c_lam_q2[j], c_lam_k2[j]], axis=0)
            o = _attn_c(lam_vecs, c_subln_gain[j][None, :], q, k, v, n_ctx, lam_init)
            w_o = c_w_o[j]

        w_router = jnp.zeros((ROUTER_ROWS, d), F32)
        w_router = w_router.at[0:N_GROUPS].set(moe_w_group[i].T).at[SUBLANES:SUBLANES + N_EXPERTS].set(moe_w_expert[i].T)
        b_router = jnp.zeros((ROUTER_ROWS, 1), F32)
        b_router = b_router.at[0:N_GROUPS, 0].set(moe_b_group[i]).at[SUBLANES:SUBLANES + N_EXPERTS, 0].set(moe_b_expert[i])

        stream, h2, rinfo, wcol, counts = _oproj_router(
            o, w_o.astype(BF16), stream, mod, ln_gain[i], ln_bias[i], w_router, b_router, alpha)

        cnt = counts[:, 0].astype(jnp.int32)
        tiles = (cnt + EXPERT_TILE - 1) // EXPERT_TILE
        tile_end = jnp.cumsum(tiles)
        row_off = (tile_end - tiles) * EXPERT_TILE
        n_used = tile_end[-1:].astype(jnp.int32)
        tile_ids = jnp.arange(n_tiles, dtype=jnp.int32)
        tile_expert = jnp.minimum(
            jnp.sum((tile_end[None, :] <= tile_ids[:, None]).astype(jnp.int32), axis=1), N_EXPERTS - 1)

        xs = _dispatch(row_off, rinfo, h2.reshape(n_tok, d), xs_rows)
        ys = _experts(tile_expert, n_used, xs, moe_w_gate, moe_w_up, moe_w_down, i)
        stream = _combine(row_off, rinfo, ys, wcol, stream, mod, ln_gain[i], ln_bias[i], alpha)

    return stream[:, n_ctx:, :]
```

```python
import functools

import numpy as np
import jax
import jax.numpy as jnp
from jax import lax
from jax.experimental import pallas as pl
from jax.experimental.pallas import tpu as pltpu

GRID_W = 64
ROPE_THETA = 10000.0
LN_EPS = 1e-5
RMS_EPS = 1e-6
NEG_INF = -1e30
N_MIXERS = 3
LOG2_E = float(np.log2(np.e))

A_HEADS, A_KV_HEADS, A_HEAD_DIM = 8, 2, 128
B_HEADS, B_KV_HEADS, B_HEAD_DIM = 16, 4, 64
WINDOW = 128
C_HEADS, C_HEAD_DIM = 8, 64
N_GROUPS, EXPERTS_PER_GROUP = 4, 8
N_EXPERTS = N_GROUPS * EXPERTS_PER_GROUP

LANES = 128
SUBLANES = 8
HALF = LANES // 2
TOKEN_BLOCK = 256
EXPERT_TILE = 256
ROUTER_ROWS = 128
ROW_UNROLL = 8
VMEM_LIMIT = 56 * 1024 * 1024

F32 = jnp.float32
BF16 = jnp.bfloat16


def _nt_dot(a, b):
    return lax.dot_general(a, b, (((1,), (1,)), ((), ())), preferred_element_type=F32)


def _dot(a, b):
    return jnp.dot(a, b, preferred_element_type=F32)


def _split_bf16(x):
    hi = x.astype(BF16)
    lo = (x - hi.astype(F32)).astype(BF16)
    return hi, lo


def _params(n_axes):
    return pltpu.CompilerParams(dimension_semantics=("arbitrary",) * n_axes, vmem_limit_bytes=VMEM_LIMIT)


def _slot_layout(n_freq):
    n_sub = LANES // (4 * n_freq)
    perm = np.zeros(LANES, np.int32)
    axis_of = np.zeros(LANES, np.int32)
    freq_of = np.zeros(LANES, np.int32)
    half_of = np.zeros(LANES, np.int32)
    sub_of = np.zeros(LANES, np.int32)
    for half in range(2):
        for sub in range(n_sub):
            for axis in range(2):
                for f in range(n_freq):
                    l = half * HALF + sub * 2 * n_freq + axis * n_freq + f
                    perm[l] = sub * 4 * n_freq + axis * 2 * n_freq + half * n_freq + f
                    axis_of[l], freq_of[l], half_of[l], sub_of[l] = axis, f, half, sub
    return perm, axis_of, freq_of, half_of, sub_of


def _rope_tables(seq, ctx, n_freq, q_scale):
    _, axis_of, freq_of, half_of, _ = _slot_layout(n_freq)
    rows = seq // GRID_W
    row = jnp.broadcast_to(jnp.arange(rows, dtype=F32)[:, None], (rows, GRID_W)).reshape(-1)
    col = jnp.broadcast_to(jnp.arange(GRID_W, dtype=F32)[None, :], (rows, GRID_W)).reshape(-1)
    inv = ROPE_THETA ** (-jnp.arange(n_freq, dtype=F32) / n_freq)
    ang = jnp.stack([row[:, None] * inv, col[:, None] * inv], axis=1)
    cos, sin = jnp.cos(ang), jnp.sin(ang)
    cos_l = cos[:, axis_of, freq_of]
    sin_l = sin[:, axis_of, freq_of] * jnp.asarray(np.where(half_of == 0, -1.0, 1.0), F32)
    cos_l = jnp.concatenate([jnp.ones((ctx, LANES), F32), cos_l], axis=0)
    sin_l = jnp.concatenate([jnp.zeros((ctx, LANES), F32), sin_l], axis=0)
    return cos_l * q_scale, sin_l * q_scale, cos_l, sin_l


def _mod_kernel(c_ref, w_ref, b_ref, o_ref):
    c = c_ref[...]
    s = c * (1.0 / (1.0 + jnp.exp(-c)))
    s_hi, s_lo = _split_bf16(s)
    w_hi, w_lo = _split_bf16(w_ref[...])
    o_ref[...] = _dot(s_hi, w_hi) + _dot(s_lo, w_hi) + _dot(s_hi, w_lo) + b_ref[...]


def _modulation(cc, w_mod, b_mod):
    depth, d, n = w_mod.shape
    rows = cc.shape[0]
    tn = n // 4
    return pl.pallas_call(
        _mod_kernel,
        grid=(depth, n // tn),
        in_specs=[
            pl.BlockSpec((rows, d), lambda l, j: (0, 0)),
            pl.BlockSpec((None, d, tn), lambda l, j: (l, 0, j)),
            pl.BlockSpec((None, 1, tn), lambda l, j: (l, 0, j)),
        ],
        out_specs=pl.BlockSpec((None, rows, tn), lambda l, j: (l, 0, j)),
        out_shape=jax.ShapeDtypeStruct((depth, rows, n), F32),
        compiler_params=_params(2),
        name="modulation",
    )(cc, w_mod, b_mod.reshape(depth, 1, n))


def _qkv_kernel(*refs, n_q, n_k, rms):
    if rms:
        x_ref, mod_ref, w_ref, cq_ref, sq_ref, ck_ref, sk_ref, gq_ref, gk_ref, q_ref, k_ref, v_ref = refs
    else:
        x_ref, mod_ref, w_ref, cq_ref, sq_ref, ck_ref, sk_ref, q_ref, k_ref, v_ref = refs
        gq_ref = gk_ref = None
    sh, sc = mod_ref[0:1, :], mod_ref[1:2, :]
    h = (x_ref[...] * (1.0 + sc) + sh).astype(BF16)

    def rope_slots(y, cos, sin, gain_ref, out_ref, n_slots):
        for j in range(n_slots):
            yj = y[:, j * LANES:(j + 1) * LANES]
            if gain_ref is not None:
                ms = jnp.mean(yj * yj, axis=-1, keepdims=True)
                yj = yj * lax.rsqrt(ms + RMS_EPS) * gain_ref[...]
            out = yj * cos + pltpu.roll(yj, HALF, 1) * sin
            out_ref[:, j * LANES:(j + 1) * LANES] = out.astype(BF16)

    nq, nk = n_q * LANES, n_k * LANES
    rope_slots(_dot(h, w_ref[:, 0:nq]), cq_ref[...], sq_ref[...], gq_ref, q_ref, n_q)
    rope_slots(_dot(h, w_ref[:, nq:nq + nk]), ck_ref[...], sk_ref[...], gk_ref, k_ref, n_k)
    v_ref[...] = _dot(h, w_ref[:, nq + nk:]).astype(BF16)


def _qkv_call(x, mod, w, tables, gains, n_q, n_k, n_v):
    bsz, t_len, d = x.shape
    n_t = t_len // TOKEN_BLOCK
    n_rows = mod.shape[0]
    rms = gains is not None
    tok = lambda b, t: (b, t, 0)
    tab_spec = pl.BlockSpec((TOKEN_BLOCK, LANES), lambda b, t: (t, 0))
    in_specs = [
        pl.BlockSpec((None, TOKEN_BLOCK, d), tok),
        pl.BlockSpec((None, 6, d), lambda b, t: (jnp.where(t == 0, bsz, b), 0, 0)),
        pl.BlockSpec(w.shape, lambda b, t: (0, 0)),
        tab_spec, tab_spec, tab_spec, tab_spec,
    ]
    args = [x, mod, w, *tables]
    if rms:
        in_specs += [pl.BlockSpec((1, LANES), lambda b, t: (0, 0))] * 2
        args += list(gains)
    del n_rows
    return pl.pallas_call(
        functools.partial(_qkv_kernel, n_q=n_q, n_k=n_k, rms=rms),
        grid=(bsz, n_t),
        in_specs=in_specs,
        out_specs=[pl.BlockSpec((None, TOKEN_BLOCK, n * LANES), tok) for n in (n_q, n_k, n_v)],
        out_shape=[jax.ShapeDtypeStruct((bsz, t_len, n * LANES), BF16) for n in (n_q, n_k, n_v)],
        compiler_params=_params(2),
        name="qkv_rope",
    )(*args)


def _softmax_pv(s, v_ext):
    e = jnp.exp2(s - jnp.max(s, axis=-1, keepdims=True))
    oe = _dot(e.astype(BF16), v_ext)
    return oe[:, :LANES], oe[:, LANES:]


def _fill_v_ext(v_ref, v_ext):
    v_ext[:, :LANES] = v_ref[...]
    v_ext[:, LANES:] = jnp.ones((v_ref.shape[0], LANES), BF16)


def _attn_a_kernel(q_ref, k_ref, v_ref, o_ref, v_ext, *, ctx, group):
    t = pl.program_id(2)
    pl.when(t == 0)(lambda: _fill_v_ext(v_ref, v_ext))

    def attend(n_keys):
        k, v = k_ref[0:n_keys, :], v_ext[0:n_keys, :]
        for g in range(group):
            num, den = _softmax_pv(_nt_dot(q_ref[:, g * LANES:(g + 1) * LANES], k), v)
            o_ref[:, g * LANES:(g + 1) * LANES] = (num / den).astype(BF16)

    pl.when(t == 0)(lambda: attend(ctx))
    pl.when(t > 0)(lambda: attend(k_ref.shape[0]))


def _attn_a(q, k, v, ctx):
    bsz, t_len, _ = q.shape
    group = A_HEADS // A_KV_HEADS
    qo_spec = pl.BlockSpec((None, TOKEN_BLOCK, group * LANES), lambda b, h, t: (b, t, h))
    kv_spec = pl.BlockSpec((None, t_len, LANES), lambda b, h, t: (b, 0, h))
    return pl.pallas_call(
        functools.partial(_attn_a_kernel, ctx=ctx, group=group),
        grid=(bsz, A_KV_HEADS, t_len // TOKEN_BLOCK),
        in_specs=[qo_spec, kv_spec, kv_spec],
        out_specs=qo_spec,
        out_shape=jax.ShapeDtypeStruct(q.shape, BF16),
        scratch_shapes=[pltpu.VMEM((t_len, 2 * LANES), BF16)],
        compiler_params=_params(3),
        name="attn_dense_gqa",
    )(q, k, v)


def _attn_b_kernel(sink_ref, q_ref, k_ref, v_ref, o_ref, v_ext, *, ctx, seq, group):
    h, t = pl.program_id(1), pl.program_id(2)
    lane = lax.broadcasted_iota(jnp.int32, (1, LANES), 1)
    first_head = (lane // (HALF // 2)) % 2 == 0
    lower = lane < HALF
    span = TOKEN_BLOCK + 2 * WINDOW

    @pl.when(t == 0)
    def _():
        v_ext[...] = jnp.where(lower, v_ref[...], jnp.ones_like(v_ref))

    def run(kk, vv, valid):
        for sl in range(group // 2):
            qs = q_ref[:, sl * LANES:(sl + 1) * LANES]
            outs = []
            for j in range(2):
                qm = jnp.where(first_head if j == 0 else jnp.logical_not(first_head), qs, jnp.zeros_like(qs))
                s = _nt_dot(qm, kk)
                if valid is not None:
                    s = jnp.concatenate([s[:, :ctx], jnp.where(valid, s[:, ctx:], NEG_INF)], axis=1)
                snk = sink_ref[h * group + sl * 2 + j] * LOG2_E
                m = jnp.maximum(jnp.max(s, axis=-1, keepdims=True), snk)
                oe = _dot(jnp.exp2(s - m).astype(BF16), vv)
                outs.append((oe, pltpu.roll(oe, HALF, 1), jnp.exp2(snk - m)))
            (oa, ra, za), (ob, rb, zb) = outs
            num = jnp.where(lower, oa, rb)
            den = jnp.where(lower, ra + za, ob + zb)
            o_ref[:, sl * LANES:(sl + 1) * LANES] = (num / den).astype(BF16)

    @pl.when(t == 0)
    def _():
        run(k_ref[0:ctx, :], v_ext[0:ctx, :], None)

    @pl.when(t > 0)
    def _():
        p0 = (t - 1) * TOKEN_BLOCK
        start = pl.multiple_of(jnp.clip(p0 - WINDOW, 0, seq - span), WINDOW)
        kk = jnp.concatenate([k_ref[0:ctx, :], k_ref[pl.ds(ctx + start, span), :]], axis=0)
        vv = jnp.concatenate([v_ext[0:ctx, :], v_ext[pl.ds(ctx + start, span), :]], axis=0)
        dist = (lax.broadcasted_iota(jnp.int32, (TOKEN_BLOCK, span), 1)
                - lax.broadcasted_iota(jnp.int32, (TOKEN_BLOCK, span), 0) + (start - p0))
        run(kk, vv, jnp.abs(dist) <= WINDOW)


def _attn_b(sink, q, k, v, ctx):
    bsz, t_len, _ = q.shape
    group = B_HEADS // B_KV_HEADS
    width = group // 2 * LANES
    qo_spec = pl.BlockSpec((None, TOKEN_BLOCK, width), lambda b, h, t, s: (b, t, h))
    kv_spec = pl.BlockSpec((None, t_len, LANES), lambda b, h, t, s: (b, 0, h))
    return pl.pallas_call(
        functools.partial(_attn_b_kernel, ctx=ctx, seq=t_len - ctx, group=group),
        grid_spec=pltpu.PrefetchScalarGridSpec(
            num_scalar_prefetch=1,
            grid=(bsz, B_KV_HEADS, t_len // TOKEN_BLOCK),
            in_specs=[qo_spec, kv_spec, kv_spec],
            out_specs=qo_spec,
            scratch_shapes=[pltpu.VMEM((t_len, LANES), BF16)],
        ),
        out_shape=jax.ShapeDtypeStruct(q.shape, BF16),
        compiler_params=_params(3),
        name="attn_window_sink",
    )(sink, q, k, v)


def _attn_c_kernel(lam_ref, gain_ref, q_ref, k_ref, v_ref, o_ref, *, ctx, lam_init):
    t = pl.program_id(2)
    lane = lax.broadcasted_iota(jnp.int32, (1, LANES), 1)
    first_map = (lane // (HALF // 2)) % 2 == 0
    lam = (jnp.exp(jnp.sum(lam_ref[0:1, :] * lam_ref[1:2, :], axis=-1, keepdims=True))
           - jnp.exp(jnp.sum(lam_ref[2:3, :] * lam_ref[3:4, :], axis=-1, keepdims=True)) + lam_init)

    def attend(n_keys):
        k, v = k_ref[0:n_keys, :], v_ref[0:n_keys, :]
        q = q_ref[...]
        zero = jnp.zeros_like(q)
        s0 = _nt_dot(jnp.where(first_map, q, zero), k)
        s1 = _nt_dot(jnp.where(first_map, zero, q), k)
        e0 = jnp.exp2(s0 - jnp.max(s0, axis=-1, keepdims=True))
        e1 = jnp.exp2(s1 - jnp.max(s1, axis=-1, keepdims=True))
        r0 = 1.0 / jnp.sum(e0, axis=-1, keepdims=True)
        r1 = lam / jnp.sum(e1, axis=-1, keepdims=True)
        o = _dot((e0 * r0 - e1 * r1).astype(BF16), v)
        y = o * lax.rsqrt(jnp.mean(o * o, axis=-1, keepdims=True) + RMS_EPS) * gain_ref[...]
        o_ref[...] = (y * (1.0 - lam_init)).astype(BF16)

    pl.when(t == 0)(lambda: attend(ctx))
    pl.when(t > 0)(lambda: attend(k_ref.shape[0]))


def _attn_c(lam_vecs, gain, q, k, v, ctx, lam_init):
    bsz, t_len, _ = q.shape
    qo_spec = pl.BlockSpec((None, TOKEN_BLOCK, LANES), lambda b, h, t: (b, t, h))
    kv_spec = pl.BlockSpec((None, t_len, LANES), lambda b, h, t: (b, 0, h))
    return pl.pallas_call(
        functools.partial(_attn_c_kernel, ctx=ctx, lam_init=lam_init),
        grid=(bsz, C_HEADS, t_len // TOKEN_BLOCK),
        in_specs=[pl.BlockSpec(lam_vecs.shape, lambda b, h, t: (0, 0)),
                  pl.BlockSpec(gain.shape, lambda b, h, t: (0, 0)),
                  qo_spec, kv_spec, kv_spec],
        out_specs=qo_spec,
        out_shape=jax.ShapeDtypeStruct(q.shape, BF16),
        compiler_params=_params(3),
        name="attn_differential",
    )(lam_vecs, gain, q, k, v)


def _layer_norm(z, gain, bias):
    mu = jnp.mean(z, axis=-1, keepdims=True)
    zc = z - mu
    var = jnp.mean(zc * zc, axis=-1, keepdims=True)
    return zc * lax.rsqrt(var + LN_EPS) * gain + bias


def _oproj_router_kernel(o_ref, wo_ref, x_ref, mod_ref, lng_ref, lnb_ref, wr_ref, br_ref,
                         xo_ref, h2_ref, ri_ref, wcol_ref, cnt_ref, carry_ref, *, alpha):
    @pl.when(jnp.logical_and(pl.program_id(0) == 0, pl.program_id(1) == 0))
    def _():
        carry_ref[...] = jnp.zeros_like(carry_ref)

    tb = x_ref.shape[0]
    y = _dot(o_ref[...], wo_ref[...])
    xn = _layer_norm(alpha * x_ref[...] + mod_ref[2:3, :] * y, lng_ref[0:1, :], lnb_ref[0:1, :])
    xo_ref[...] = xn
    h2 = xn * (1.0 + mod_ref[4:5, :]) + mod_ref[3:4, :]
    h_hi, h_lo = _split_bf16(h2)
    h2_ref[...] = h_hi
    w_hi, w_lo = _split_bf16(wr_ref[...])
    logits = _nt_dot(w_hi, h_hi) + _nt_dot(w_hi, h_lo) + _nt_dot(w_lo, h_hi) + br_ref[...]

    row8 = lax.broadcasted_iota(jnp.int32, (SUBLANES, tb), 0)
    glog = jnp.where(row8 < N_GROUPS, logits[0:SUBLANES, :], NEG_INF)
    gmax = jnp.max(glog, axis=0, keepdims=True)
    gsel = jnp.min(jnp.where(glog == gmax, row8, SUBLANES), axis=0, keepdims=True)
    gprob = 1.0 / jnp.sum(jnp.exp(glog - gmax), axis=0, keepdims=True)
    elog = jnp.zeros((SUBLANES, tb), F32)
    for g in range(N_GROUPS):
        elog = jnp.where(gsel == g, logits[(g + 1) * SUBLANES:(g + 2) * SUBLANES, :], elog)
    v1 = jnp.max(elog, axis=0, keepdims=True)
    i1 = jnp.min(jnp.where(elog == v1, row8, SUBLANES), axis=0, keepdims=True)
    elog2 = jnp.where(row8 == i1, NEG_INF, elog)
    v2 = jnp.max(elog2, axis=0, keepdims=True)
    i2 = jnp.min(jnp.where(elog2 == v2, row8, SUBLANES), axis=0, keepdims=True)
    t21 = jnp.exp(v2 - v1)
    w1 = gprob / (1.0 + t21)
    w2 = gprob * t21 / (1.0 + t21)
    e1 = gsel * EXPERTS_PER_GROUP + i1
    e2 = gsel * EXPERTS_PER_GROUP + i2

    rowe = lax.broadcasted_iota(jnp.int32, (N_EXPERTS, tb), 0)
    oh1, oh2 = rowe == e1, rowe == e2
    oh = jnp.where(jnp.logical_or(oh1, oh2), 1.0, 0.0)
    before = (lax.broadcasted_iota(jnp.int32, (tb, tb), 0) < lax.broadcasted_iota(jnp.int32, (tb, tb), 1))
    base = _dot(oh.astype(BF16), jnp.where(before, 1.0, 0.0).astype(BF16)) + carry_ref[:, 0:1]
    rank1 = jnp.sum(jnp.where(oh1, base, 0.0), axis=0, keepdims=True).astype(jnp.int32)
    rank2 = jnp.sum(jnp.where(oh2, base, 0.0), axis=0, keepdims=True).astype(jnp.int32)
    carry_ref[...] = carry_ref[...] + jnp.sum(oh, axis=1, keepdims=True)
    cnt_ref[...] = carry_ref[...]

    ri_ref[...] = jnp.where(row8 == 0, e1, jnp.where(row8 == 1, e2, jnp.where(row8 == 2, rank1,
                            jnp.where(row8 == 3, rank2, 0))))
    rowl = lax.broadcasted_iota(jnp.int32, (LANES, tb), 0)
    wcol_ref[...] = jnp.where(rowl == 0, w1, jnp.where(rowl == 1, w2, 0.0)).T


def _oproj_router(o, w_o, x, mod, lng, lnb, w_router, b_router, alpha):
    bsz, t_len, d = x.shape
    n_t = t_len // TOKEN_BLOCK
    n_blk = bsz * n_t
    tok = lambda b, t: (b, t, 0)
    blk = lambda b, t: (b * n_t + t, 0, 0)
    const2 = lambda b, t: (0, 0)
    return pl.pallas_call(
        functools.partial(_oproj_router_kernel, alpha=alpha),
        grid=(bsz, n_t),
        in_specs=[
            pl.BlockSpec((None, TOKEN_BLOCK, o.shape[-1]), tok),
            pl.BlockSpec(w_o.shape, const2),
            pl.BlockSpec((None, TOKEN_BLOCK, d), tok),
            pl.BlockSpec((None, 6, d), lambda b, t: (jnp.where(t == 0, bsz, b), 0, 0)),
            pl.BlockSpec(lng.shape, const2),
            pl.BlockSpec(lnb.shape, const2),
            pl.BlockSpec(w_router.shape, const2),
            pl.BlockSpec(b_router.shape, const2),
        ],
        out_specs=[
            pl.BlockSpec((None, TOKEN_BLOCK, d), tok),
            pl.BlockSpec((None, TOKEN_BLOCK, d), tok),
            pl.BlockSpec((None, SUBLANES, TOKEN_BLOCK), blk),
            pl.BlockSpec((None, TOKEN_BLOCK, LANES), blk),
            pl.BlockSpec((N_EXPERTS, LANES), const2),
        ],
        out_shape=[
            jax.ShapeDtypeStruct((bsz, t_len, d), F32),
            jax.ShapeDtypeStruct((bsz, t_len, d), BF16),
            jax.ShapeDtypeStruct((n_blk, SUBLANES, TOKEN_BLOCK), jnp.int32),
            jax.ShapeDtypeStruct((n_blk, TOKEN_BLOCK, LANES), F32),
            jax.ShapeDtypeStruct((N_EXPERTS, LANES), F32),
        ],
        scratch_shapes=[pltpu.VMEM((N_EXPERTS, LANES), F32)],
        compiler_params=_params(2),
        name="oproj_ln_router",
    )(o, w_o, x, mod, lng, lnb, w_router, b_router)


def _for_each_row(n_rows, fn):
    def chunk(c, carry):
        for u in range(ROW_UNROLL):
            fn(c * ROW_UNROLL + u)
        return carry

    lax.fori_loop(0, n_rows // ROW_UNROLL, chunk, 0)


def _expert_row(off_ref, ri_ref, k, r):
    return off_ref[ri_ref[k, r]] + ri_ref[2 + k, r]


def _dispatch_kernel(off_ref, tiles_ref, ri_ref, h_ref, xs_ref, rows, zeros, sems):
    tb = h_ref.shape[0]

    @pl.when(pl.program_id(0) == 0)
    def _():
        zeros[...] = jnp.zeros_like(zeros)

        def last_tile(e):
            start = pl.multiple_of(off_ref[e] + (tiles_ref[e] - 1) * EXPERT_TILE, EXPERT_TILE)
            return pltpu.make_async_copy(zeros, xs_ref.at[pl.ds(start, EXPERT_TILE)], sems.at[2])

        for e in range(N_EXPERTS):
            pl.when(tiles_ref[e] > 0)(lambda e=e: last_tile(e).start())
        for e in range(N_EXPERTS):
            pl.when(tiles_ref[e] > 0)(lambda e=e: last_tile(e).wait())

        def spare_tile(i):
            return pltpu.make_async_copy(
                zeros, xs_ref.at[pl.ds(pl.multiple_of(i * EXPERT_TILE, EXPERT_TILE), EXPERT_TILE)], sems.at[2])

        first_spare = off_ref[N_EXPERTS - 1] // EXPERT_TILE + tiles_ref[N_EXPERTS - 1]
        n_tiles = xs_ref.shape[0] // EXPERT_TILE
        lax.fori_loop(first_spare, n_tiles, lambda i, c: (spare_tile(i).start(), c)[1], 0)
        lax.fori_loop(first_spare, n_tiles, lambda i, c: (spare_tile(i).wait(), c)[1], 0)

    rows[...] = h_ref[...].astype(F32)

    def start(r):
        for k in range(2):
            pltpu.make_async_copy(rows.at[pl.ds(r, 1)], xs_ref.at[pl.ds(_expert_row(off_ref, ri_ref, k, r), 1)],
                                  sems.at[k]).start(priority=k)

    _for_each_row(tb, start)
    for k in range(2):
        pltpu.make_async_copy(rows, xs_ref.at[pl.ds(0, tb)], sems.at[k]).wait()


def _dispatch(row_off, tiles, rinfo, h2, n_rows):
    n_tok, d = h2.shape
    n_blk = n_tok // TOKEN_BLOCK
    return pl.pallas_call(
        _dispatch_kernel,
        grid_spec=pltpu.PrefetchScalarGridSpec(
            num_scalar_prefetch=2,
            grid=(n_blk,),
            in_specs=[
                pl.BlockSpec((None, SUBLANES, TOKEN_BLOCK), lambda i, off, nt: (i, 0, 0), memory_space=pltpu.SMEM),
                pl.BlockSpec((TOKEN_BLOCK, d), lambda i, off, nt: (i, 0)),
            ],
            out_specs=pl.BlockSpec(memory_space=pl.ANY),
            scratch_shapes=[pltpu.VMEM((TOKEN_BLOCK, d), F32), pltpu.VMEM((EXPERT_TILE, d), F32),
                            pltpu.SemaphoreType.DMA((3,))],
        ),
        out_shape=jax.ShapeDtypeStruct((n_rows, d), F32),
        compiler_params=_params(1),
        name="moe_dispatch",
    )(row_off, tiles, rinfo, h2)


def _experts_kernel(te_ref, nu_ref, nxt_ref, xs_ref, wg_hbm, wu_hbm, wd_hbm, ys_ref,
                    wg_f, wu_f, wd_f, wg_s, wu_s, wd_s, slot_ref, sems, *, layer):
    t = pl.program_id(0)

    def fetch(e, slot):
        g, i = e // EXPERTS_PER_GROUP, e % EXPERTS_PER_GROUP
        return [pltpu.make_async_copy(w.at[layer, g, i], buf.at[slot], sems.at[slot, n])
                for n, (w, buf) in enumerate(((wg_hbm, wg_f), (wu_hbm, wu_f), (wd_hbm, wd_f)))]

    @pl.when(t < nu_ref[0])
    def _():
        e = te_ref[t]

        @pl.when(t == 0)
        def _():
            slot_ref[0] = 0
            for c in fetch(e, 0):
                c.start()

        @pl.when(jnp.logical_or(t == 0, e != te_ref[jnp.maximum(t - 1, 0)]))
        def _():
            slot = slot_ref[0]
            for c in fetch(e, slot):
                c.wait()
            nxt = nxt_ref[e]

            @pl.when(nxt >= 0)
            def _():
                for c in fetch(nxt, 1 - slot):
                    c.start()

            wg_s[...] = wg_f[slot].astype(BF16)
            wu_s[...] = wu_f[slot].astype(BF16)
            wd_s[...] = wd_f[slot].astype(BF16)
            slot_ref[0] = 1 - slot

        x = xs_ref[...].astype(BF16)
        a = _dot(x, wg_s[...])
        u = _dot(x, wu_s[...])
        hid = a * (1.0 / (1.0 + jnp.exp(-a))) * u
        ys_ref[...] = _dot(hid.astype(BF16), wd_s[...])

    @pl.when(t >= nu_ref[0])
    def _():
        ys_ref[...] = jnp.zeros_like(ys_ref)


def _experts(tile_expert, n_used, next_expert, xs, w_gate, w_up, w_down, layer):
    n_rows, d = xs.shape
    ff = w_gate.shape[-1]
    n_tiles = n_rows // EXPERT_TILE
    hbm = pl.BlockSpec(memory_space=pl.ANY)
    return pl.pallas_call(
        functools.partial(_experts_kernel, layer=layer),
        grid_spec=pltpu.PrefetchScalarGridSpec(
            num_scalar_prefetch=3,
            grid=(n_tiles,),
            in_specs=[
                pl.BlockSpec((EXPERT_TILE, d), lambda t, te, nu, nx: (jnp.minimum(t, nu[0] - 1), 0)),
                hbm, hbm, hbm,
            ],
            out_specs=pl.BlockSpec((EXPERT_TILE, d), lambda t, te, nu, nx: (t, 0)),
            scratch_shapes=[
                pltpu.VMEM((2, d, ff), F32), pltpu.VMEM((2, d, ff), F32), pltpu.VMEM((2, ff, d), F32),
                pltpu.VMEM((d, ff), BF16), pltpu.VMEM((d, ff), BF16), pltpu.VMEM((ff, d), BF16),
                pltpu.SMEM((1,), jnp.int32), pltpu.SemaphoreType.DMA((2, 3)),
            ],
        ),
        out_shape=jax.ShapeDtypeStruct((n_rows, d), F32),
        compiler_params=_params(1),
        name="moe_experts",
    )(tile_expert, n_used, next_expert, xs, w_gate, w_up, w_down)


def _combine_kernel(off_ref, ri_ref, ri_next_ref, ys_ref, wcol_ref, x_ref, mod_ref, lng_ref, lnb_ref, o_ref,
                    buf, sems, *, alpha):
    tb = x_ref.shape[0]
    step = pl.program_id(0) * pl.num_programs(1) + pl.program_id(1)
    n_steps = pl.num_programs(0) * pl.num_programs(1)
    slot = step % 2

    def gather(ri, into):
        def start(r):
            for k in range(2):
                pltpu.make_async_copy(ys_ref.at[pl.ds(_expert_row(off_ref, ri, k, r), 1)],
                                      buf.at[into, k, pl.ds(r, 1)], sems.at[into, k]).start(priority=k)

        _for_each_row(tb, start)

    pl.when(step == 0)(lambda: gather(ri_ref, 0))
    pl.when(step + 1 < n_steps)(lambda: gather(ri_next_ref, 1 - slot))
    for k in range(2):
        pltpu.make_async_copy(ys_ref.at[pl.ds(0, tb)], buf.at[slot, k], sems.at[slot, k]).wait()
    f = wcol_ref[:, 0:1] * buf[slot, 0] + wcol_ref[:, 1:2] * buf[slot, 1]
    o_ref[...] = _layer_norm(alpha * x_ref[...] + mod_ref[5:6, :] * f, lng_ref[1:2, :], lnb_ref[1:2, :])


def _combine(row_off, rinfo, ys, wcol, x, mod, lng, lnb, alpha, first_block):
    bsz, t_len, d = x.shape
    n_t = t_len // TOKEN_BLOCK
    n_out = n_t - first_block
    n_steps = bsz * n_out

    def next_blk(b, t, off):
        s1 = jnp.minimum(b * n_out + t + 1, n_steps - 1)
        return ((s1 // n_out) * n_t + s1 % n_out + first_block, 0, 0)

    tok = lambda b, t, off: (b, t + first_block, 0)
    blk = lambda b, t, off: (b * n_t + t + first_block, 0, 0)
    const2 = lambda b, t, off: (0, 0)
    ri_block = (None, SUBLANES, TOKEN_BLOCK)
    return pl.pallas_call(
        functools.partial(_combine_kernel, alpha=alpha),
        grid_spec=pltpu.PrefetchScalarGridSpec(
            num_scalar_prefetch=1,
            grid=(bsz, n_out),
            in_specs=[
                pl.BlockSpec(ri_block, blk, memory_space=pltpu.SMEM),
                pl.BlockSpec(ri_block, next_blk, memory_space=pltpu.SMEM),
                pl.BlockSpec(memory_space=pl.ANY),
                pl.BlockSpec((None, TOKEN_BLOCK, LANES), blk),
                pl.BlockSpec((None, TOKEN_BLOCK, d), tok),
                pl.BlockSpec((None, 6, d), lambda b, t, off: (jnp.where(t + first_block == 0, bsz, b), 0, 0)),
                pl.BlockSpec(lng.shape, const2),
                pl.BlockSpec(lnb.shape, const2),
            ],
            out_specs=pl.BlockSpec((None, TOKEN_BLOCK, d), lambda b, t, off: (b, t, 0)),
            scratch_shapes=[pltpu.VMEM((2, 2, TOKEN_BLOCK, d), F32), pltpu.SemaphoreType.DMA((2, 2))],
        ),
        out_shape=jax.ShapeDtypeStruct((bsz, n_out * TOKEN_BLOCK, d), F32),
        compiler_params=_params(2),
        name="moe_combine_ln",
    )(row_off, rinfo, rinfo, ys, wcol, x, mod, lng, lnb)


def _qkv_columns(kind):
    if kind == 0:
        perm, *_ = _slot_layout(A_HEAD_DIM // 4)
        nq, nk = A_HEADS, A_KV_HEADS
        q = np.concatenate([s * LANES + perm for s in range(nq)])
        k = np.concatenate([nq * LANES + s * LANES + perm for s in range(nk)])
        v = np.arange((nq + nk) * LANES, (nq + 2 * nk) * LANES)
        return np.concatenate([q, k, v]), nq, nk, nk
    perm, axis_of, freq_of, half_of, _ = _slot_layout(B_HEAD_DIM // 4)
    nf = B_HEAD_DIM // 4
    if kind == 1:
        nq = B_HEADS // 2
        q = np.concatenate([s * LANES + perm for s in range(nq)])
        k_base = B_HEADS * B_HEAD_DIM
        v_base = k_base + B_KV_HEADS * B_HEAD_DIM
        in_head = axis_of * 2 * nf + half_of * nf + freq_of
        k = np.concatenate([k_base + h * B_HEAD_DIM + in_head for h in range(B_KV_HEADS)])
        v = np.concatenate([v_base + h * B_HEAD_DIM + np.arange(LANES) % B_HEAD_DIM for h in range(B_KV_HEADS)])
        return np.concatenate([q, k, v]), nq, B_KV_HEADS, B_KV_HEADS
    nq = C_HEADS
    q = np.concatenate([s * LANES + perm for s in range(nq)])
    k = np.concatenate([nq * LANES + s * LANES + perm for s in range(nq)])
    v = np.arange(2 * nq * LANES, 3 * nq * LANES)
    return np.concatenate([q, k, v]), nq, nq, nq


def kernel(x, c, ctx, c_ctx, w_mod, b_mod, ln_gain, ln_bias, a_w_qkv, a_q_gain, a_k_gain, a_w_o, b_w_qkv, b_sink, b_w_o, c_w_qkv, c_lam_q1, c_lam_k1, c_lam_q2, c_lam_k2, c_subln_gain, c_w_o, moe_w_group, moe_b_group, moe_w_expert, moe_b_expert, moe_w_gate, moe_w_up, moe_w_down):
    bsz, seq, d = x.shape
    n_ctx = ctx.shape[1]
    depth = w_mod.shape[0]
    assert n_ctx == TOKEN_BLOCK and seq % TOKEN_BLOCK == 0 and seq >= TOKEN_BLOCK + 2 * WINDOW
    t_len = n_ctx + seq
    n_tok = bsz * t_len
    alpha = (2.0 * depth) ** 0.25

    mod_rows = -(-(bsz + 1) // SUBLANES) * SUBLANES
    cc = jnp.concatenate([c, c_ctx[None, :], jnp.zeros((mod_rows - bsz - 1, d), F32)], axis=0)
    mod_all = _modulation(cc, w_mod, b_mod).reshape(depth, mod_rows, 6, d)

    xs_rows = 2 * n_tok + N_EXPERTS * EXPERT_TILE
    xs_rows = -(-xs_rows // EXPERT_TILE) * EXPERT_TILE
    n_tiles = xs_rows // EXPERT_TILE

    stream = jnp.concatenate([ctx, x], axis=1)
    tables = {hd: _rope_tables(seq, n_ctx, hd // 4, hd ** -0.5 * LOG2_E) for hd in (A_HEAD_DIM, B_HEAD_DIM)}
    perm_a = _slot_layout(A_HEAD_DIM // 4)[0]

    for i in range(depth):
        kind, j = i % N_MIXERS, i // N_MIXERS
        mod = mod_all[i]
        cols, n_q, n_k, n_v = _qkv_columns(kind)
        if kind == 0:
            w = a_w_qkv[j][:, cols].astype(BF16)
            gains = (a_q_gain[j][perm_a][None, :], a_k_gain[j][perm_a][None, :])
            q, k, v = _qkv_call(stream, mod, w, tables[A_HEAD_DIM], gains, n_q, n_k, n_v)
            o = _attn_a(q, k, v, n_ctx)
            w_o = a_w_o[j]
        elif kind == 1:
            w = b_w_qkv[j][:, cols].astype(BF16)
            q, k, v = _qkv_call(stream, mod, w, tables[B_HEAD_DIM], None, n_q, n_k, n_v)
            o = _attn_b(b_sink[j], q, k, v, n_ctx)
            w_o = b_w_o[j]
        else:
            w = c_w_qkv[j][:, cols].astype(BF16)
            q, k, v = _qkv_call(stream, mod, w, tables[C_HEAD_DIM], None, n_q, n_k, n_v)
            lam_init = 0.8 - 0.6 * float(np.exp(-0.3 * i))
            lam_vecs = jnp.stack([c_lam_q1[j], c_lam_k1[j], c_lam_q2[j], c_lam_k2[j]], axis=0)
            o = _attn_c(lam_vecs, c_subln_gain[j][None, :], q, k, v, n_ctx, lam_init)
            w_o = c_w_o[j]

        w_router = jnp.zeros((ROUTER_ROWS, d), F32)
        w_router = w_router.at[0:N_GROUPS].set(moe_w_group[i].T).at[SUBLANES:SUBLANES + N_EXPERTS].set(moe_w_expert[i].T)
        b_router = jnp.zeros((ROUTER_ROWS, 1), F32)
        b_router = b_router.at[0:N_GROUPS, 0].set(moe_b_group[i]).at[SUBLANES:SUBLANES + N_EXPERTS, 0].set(moe_b_expert[i])

        stream, h2, rinfo, wcol, counts = _oproj_router(
            o, w_o.astype(BF16), stream, mod, ln_gain[i], ln_bias[i], w_router, b_router, alpha)

        cnt = counts[:, 0].astype(jnp.int32)
        tiles = (cnt + EXPERT_TILE - 1) // EXPERT_TILE
        tile_end = jnp.cumsum(tiles)
        row_off = (tile_end - tiles) * EXPERT_TILE
        n_used = tile_end[-1:].astype(jnp.int32)
        tile_ids = jnp.arange(n_tiles, dtype=jnp.int32)
        tile_expert = jnp.minimum(
            jnp.sum((tile_end[None, :] <= tile_ids[:, None]).astype(jnp.int32), axis=1), N_EXPERTS - 1)

        ids = jnp.arange(N_EXPERTS, dtype=jnp.int32)
        later_used = jnp.logical_and(ids[None, :] > ids[:, None], tiles[None, :] > 0)
        next_expert = jnp.min(jnp.where(later_used, ids[None, :], N_EXPERTS), axis=1)
        next_expert = jnp.where(next_expert == N_EXPERTS, -1, next_expert).astype(jnp.int32)

        xs = _dispatch(row_off, tiles, rinfo, h2.reshape(n_tok, d), xs_rows)
        ys = _experts(tile_expert, n_used, next_expert, xs, moe_w_gate, moe_w_up, moe_w_down, i)
        last = i == depth - 1
        stream = _combine(row_off, rinfo, ys, wcol, stream, mod, ln_gain[i], ln_bias[i], alpha, 1 if last else 0)

    return stream
```

```python
import functools

import numpy as np
import jax
import jax.numpy as jnp
from jax import lax
from jax.experimental import pallas as pl
from jax.experimental.pallas import tpu as pltpu

GRID_W = 64
ROPE_THETA = 10000.0
LN_EPS = 1e-5
RMS_EPS = 1e-6
NEG_INF = -1e30
N_MIXERS = 3
LOG2_E = float(np.log2(np.e))

A_HEADS, A_KV_HEADS, A_HEAD_DIM = 8, 2, 128
B_HEADS, B_KV_HEADS, B_HEAD_DIM = 16, 4, 64
WINDOW = 128
C_HEADS, C_HEAD_DIM = 8, 64
N_GROUPS, EXPERTS_PER_GROUP = 4, 8
N_EXPERTS = N_GROUPS * EXPERTS_PER_GROUP

LANES = 128
SUBLANES = 8
HALF = LANES // 2
TOKEN_BLOCK = 256
EXPERT_TILE = 256
ROUTER_ROWS = 128
ROW_UNROLL = 8
VMEM_LIMIT = 56 * 1024 * 1024

F32 = jnp.float32
BF16 = jnp.bfloat16


def _nt_dot(a, b):
    return lax.dot_general(a, b, (((1,), (1,)), ((), ())), preferred_element_type=F32)


def _dot(a, b):
    return jnp.dot(a, b, preferred_element_type=F32)


def _split_bf16(x):
    hi = x.astype(BF16)
    lo = (x - hi.astype(F32)).astype(BF16)
    return hi, lo


def _params(n_axes):
    return pltpu.CompilerParams(dimension_semantics=("arbitrary",) * n_axes, vmem_limit_bytes=VMEM_LIMIT)


def _slot_layout(n_freq):
    n_sub = LANES // (4 * n_freq)
    perm = np.zeros(LANES, np.int32)
    axis_of = np.zeros(LANES, np.int32)
    freq_of = np.zeros(LANES, np.int32)
    half_of = np.zeros(LANES, np.int32)
    sub_of = np.zeros(LANES, np.int32)
    for half in range(2):
        for sub in range(n_sub):
            for axis in range(2):
                for f in range(n_freq):
                    l = half * HALF + sub * 2 * n_freq + axis * n_freq + f
                    perm[l] = sub * 4 * n_freq + axis * 2 * n_freq + half * n_freq + f
                    axis_of[l], freq_of[l], half_of[l], sub_of[l] = axis, f, half, sub
    return perm, axis_of, freq_of, half_of, sub_of


def _rope_tables(seq, ctx, n_freq, q_scale):
    _, axis_of, freq_of, half_of, _ = _slot_layout(n_freq)
    rows = seq // GRID_W
    row = jnp.broadcast_to(jnp.arange(rows, dtype=F32)[:, None], (rows, GRID_W)).reshape(-1)
    col = jnp.broadcast_to(jnp.arange(GRID_W, dtype=F32)[None, :], (rows, GRID_W)).reshape(-1)
    inv = ROPE_THETA ** (-jnp.arange(n_freq, dtype=F32) / n_freq)
    ang = jnp.stack([row[:, None] * inv, col[:, None] * inv], axis=1)
    cos, sin = jnp.cos(ang), jnp.sin(ang)
    cos_l = cos[:, axis_of, freq_of]
    sin_l = sin[:, axis_of, freq_of] * jnp.asarray(np.where(half_of == 0, -1.0, 1.0), F32)
    cos_l = jnp.concatenate([jnp.ones((ctx, LANES), F32), cos_l], axis=0)
    sin_l = jnp.concatenate([jnp.zeros((ctx, LANES), F32), sin_l], axis=0)
    return cos_l * q_scale, sin_l * q_scale, cos_l, sin_l


def _mod_kernel(c_ref, w_ref, b_ref, o_ref):
    c = c_ref[...]
    s = c * (1.0 / (1.0 + jnp.exp(-c)))
    s_hi, s_lo = _split_bf16(s)
    w_hi, w_lo = _split_bf16(w_ref[...])
    o_ref[...] = _dot(s_hi, w_hi) + _dot(s_lo, w_hi) + _dot(s_hi, w_lo) + b_ref[...]


def _modulation(cc, w_mod, b_mod):
    depth, d, n = w_mod.shape
    rows = cc.shape[0]
    tn = n // 4
    return pl.pallas_call(
        _mod_kernel,
        grid=(depth, n // tn),
        in_specs=[
            pl.BlockSpec((rows, d), lambda l, j: (0, 0)),
            pl.BlockSpec((None, d, tn), lambda l, j: (l, 0, j)),
            pl.BlockSpec((None, 1, tn), lambda l, j: (l, 0, j)),
        ],
        out_specs=pl.BlockSpec((None, rows, tn), lambda l, j: (l, 0, j)),
        out_shape=jax.ShapeDtypeStruct((depth, rows, n), F32),
        compiler_params=_params(2),
        name="modulation",
    )(cc, w_mod, b_mod.reshape(depth, 1, n))


def _qkv_kernel(*refs, n_q, n_k, rms):
    if rms:
        x_ref, mod_ref, w_ref, cq_ref, sq_ref, ck_ref, sk_ref, gq_ref, gk_ref, q_ref, k_ref, v_ref = refs
    else:
        x_ref, mod_ref, w_ref, cq_ref, sq_ref, ck_ref, sk_ref, q_ref, k_ref, v_ref = refs
        gq_ref = gk_ref = None
    sh, sc = mod_ref[0:1, :], mod_ref[1:2, :]
    h = (x_ref[...] * (1.0 + sc) + sh).astype(BF16)

    def rope_slots(y, cos, sin, gain_ref, out_ref, n_slots):
        for j in range(n_slots):
            yj = y[:, j * LANES:(j + 1) * LANES]
            if gain_ref is not None:
                ms = jnp.mean(yj * yj, axis=-1, keepdims=True)
                yj = yj * lax.rsqrt(ms + RMS_EPS) * gain_ref[...]
            out = yj * cos + pltpu.roll(yj, HALF, 1) * sin
            out_ref[:, j * LANES:(j + 1) * LANES] = out.astype(BF16)

    nq, nk = n_q * LANES, n_k * LANES
    rope_slots(_dot(h, w_ref[:, 0:nq]), cq_ref[...], sq_ref[...], gq_ref, q_ref, n_q)
    rope_slots(_dot(h, w_ref[:, nq:nq + nk]), ck_ref[...], sk_ref[...], gk_ref, k_ref, n_k)
    v_ref[...] = _dot(h, w_ref[:, nq + nk:]).astype(BF16)


def _qkv_call(x, mod, w, tables, gains, n_q, n_k, n_v):
    bsz, t_len, d = x.shape
    n_t = t_len // TOKEN_BLOCK
    n_rows = mod.shape[0]
    rms = gains is not None
    tok = lambda b, t: (b, t, 0)
    tab_spec = pl.BlockSpec((TOKEN_BLOCK, LANES), lambda b, t: (t, 0))
    in_specs = [
        pl.BlockSpec((None, TOKEN_BLOCK, d), tok),
        pl.BlockSpec((None, 6, d), lambda b, t: (jnp.where(t == 0, bsz, b), 0, 0)),
        pl.BlockSpec(w.shape, lambda b, t: (0, 0)),
        tab_spec, tab_spec, tab_spec, tab_spec,
    ]
    args = [x, mod, w, *tables]
    if rms:
        in_specs += [pl.BlockSpec((1, LANES), lambda b, t: (0, 0))] * 2
        args += list(gains)
    del n_rows
    return pl.pallas_call(
        functools.partial(_qkv_kernel, n_q=n_q, n_k=n_k, rms=rms),
        grid=(bsz, n_t),
        in_specs=in_specs,
        out_specs=[pl.BlockSpec((None, TOKEN_BLOCK, n * LANES), tok) for n in (n_q, n_k, n_v)],
        out_shape=[jax.ShapeDtypeStruct((bsz, t_len, n * LANES), BF16) for n in (n_q, n_k, n_v)],
        compiler_params=_params(2),
        name="qkv_rope",
    )(*args)


def _softmax_pv(s, v_ext):
    e = jnp.exp2(s - jnp.max(s, axis=-1, keepdims=True))
    oe = _dot(e.astype(BF16), v_ext)
    return oe[:, :LANES], oe[:, LANES:]


def _fill_v_ext(v_ref, v_ext):
    v_ext[:, :LANES] = v_ref[...]
    v_ext[:, LANES:] = jnp.ones((v_ref.shape[0], LANES), BF16)


def _attn_a_kernel(q_ref, k_ref, v_ref, o_ref, v_ext, *, ctx, group):
    t = pl.program_id(2)
    pl.when(t == 0)(lambda: _fill_v_ext(v_ref, v_ext))

    def attend(n_keys):
        k, v = k_ref[0:n_keys, :], v_ext[0:n_keys, :]
        for g in range(group):
            num, den = _softmax_pv(_nt_dot(q_ref[:, g * LANES:(g + 1) * LANES], k), v)
            o_ref[:, g * LANES:(g + 1) * LANES] = (num / den).astype(BF16)

    pl.when(t == 0)(lambda: attend(ctx))
    pl.when(t > 0)(lambda: attend(k_ref.shape[0]))


def _attn_a(q, k, v, ctx):
    bsz, t_len, _ = q.shape
    group = A_HEADS // A_KV_HEADS
    qo_spec = pl.BlockSpec((None, TOKEN_BLOCK, group * LANES), lambda b, h, t: (b, t, h))
    kv_spec = pl.BlockSpec((None, t_len, LANES), lambda b, h, t: (b, 0, h))
    return pl.pallas_call(
        functools.partial(_attn_a_kernel, ctx=ctx, group=group),
        grid=(bsz, A_KV_HEADS, t_len // TOKEN_BLOCK),
        in_specs=[qo_spec, kv_spec, kv_spec],
        out_specs=qo_spec,
        out_shape=jax.ShapeDtypeStruct(q.shape, BF16),
        scratch_shapes=[pltpu.VMEM((t_len, 2 * LANES), BF16)],
        compiler_params=_params(3),
        name="attn_dense_gqa",
    )(q, k, v)


def _attn_b_kernel(sink_ref, q_ref, k_ref, v_ref, o_ref, v_ext, *, ctx, seq, group):
    h, t = pl.program_id(1), pl.program_id(2)
    lane = lax.broadcasted_iota(jnp.int32, (1, LANES), 1)
    first_head = (lane // (HALF // 2)) % 2 == 0
    lower = lane < HALF
    span = TOKEN_BLOCK + 2 * WINDOW

    @pl.when(t == 0)
    def _():
        v_ext[...] = jnp.where(lower, v_ref[...], jnp.ones_like(v_ref))

    def run(kk, vv, valid):
        for sl in range(group // 2):
            qs = q_ref[:, sl * LANES:(sl + 1) * LANES]
            outs = []
            for j in range(2):
                qm = jnp.where(first_head if j == 0 else jnp.logical_not(first_head), qs, jnp.zeros_like(qs))
                s = _nt_dot(qm, kk)
                if valid is not None:
                    s = jnp.concatenate([s[:, :ctx], jnp.where(valid, s[:, ctx:], NEG_INF)], axis=1)
                snk = sink_ref[h * group + sl * 2 + j] * LOG2_E
                m = jnp.maximum(jnp.max(s, axis=-1, keepdims=True), snk)
                oe = _dot(jnp.exp2(s - m).astype(BF16), vv)
                outs.append((oe, pltpu.roll(oe, HALF, 1), jnp.exp2(snk - m)))
            (oa, ra, za), (ob, rb, zb) = outs
            num = jnp.where(lower, oa, rb)
            den = jnp.where(lower, ra + za, ob + zb)
            o_ref[:, sl * LANES:(sl + 1) * LANES] = (num / den).astype(BF16)

    @pl.when(t == 0)
    def _():
        run(k_ref[0:ctx, :], v_ext[0:ctx, :], None)

    @pl.when(t > 0)
    def _():
        p0 = (t - 1) * TOKEN_BLOCK
        start = pl.multiple_of(jnp.clip(p0 - WINDOW, 0, seq - span), WINDOW)
        kk = jnp.concatenate([k_ref[0:ctx, :], k_ref[pl.ds(ctx + start, span), :]], axis=0)
        vv = jnp.concatenate([v_ext[0:ctx, :], v_ext[pl.ds(ctx + start, span), :]], axis=0)
        dist = (lax.broadcasted_iota(jnp.int32, (TOKEN_BLOCK, span), 1)
                - lax.broadcasted_iota(jnp.int32, (TOKEN_BLOCK, span), 0) + (start - p0))
        run(kk, vv, jnp.abs(dist) <= WINDOW)


def _attn_b(sink, q, k, v, ctx):
    bsz, t_len, _ = q.shape
    group = B_HEADS // B_KV_HEADS
    width = group // 2 * LANES
    qo_spec = pl.BlockSpec((None, TOKEN_BLOCK, width), lambda b, h, t, s: (b, t, h))
    kv_spec = pl.BlockSpec((None, t_len, LANES), lambda b, h, t, s: (b, 0, h))
    return pl.pallas_call(
        functools.partial(_attn_b_kernel, ctx=ctx, seq=t_len - ctx, group=group),
        grid_spec=pltpu.PrefetchScalarGridSpec(
            num_scalar_prefetch=1,
            grid=(bsz, B_KV_HEADS, t_len // TOKEN_BLOCK),
            in_specs=[qo_spec, kv_spec, kv_spec],
            out_specs=qo_spec,
            scratch_shapes=[pltpu.VMEM((t_len, LANES), BF16)],
        ),
        out_shape=jax.ShapeDtypeStruct(q.shape, BF16),
        compiler_params=_params(3),
        name="attn_window_sink",
    )(sink, q, k, v)


def _attn_c_kernel(lam_ref, gain_ref, q_ref, k_ref, v_ref, o_ref, *, ctx, lam_init):
    t = pl.program_id(2)
    lane = lax.broadcasted_iota(jnp.int32, (1, LANES), 1)
    first_map = (lane // (HALF // 2)) % 2 == 0
    lam = (jnp.exp(jnp.sum(lam_ref[0:1, :] * lam_ref[1:2, :], axis=-1, keepdims=True))
           - jnp.exp(jnp.sum(lam_ref[2:3, :] * lam_ref[3:4, :], axis=-1, keepdims=True)) + lam_init)

    def attend(n_keys):
        k, v = k_ref[0:n_keys, :], v_ref[0:n_keys, :]
        q = q_ref[...]
        zero = jnp.zeros_like(q)
        s0 = _nt_dot(jnp.where(first_map, q, zero), k)
        s1 = _nt_dot(jnp.where(first_map, zero, q), k)
        e0 = jnp.exp2(s0 - jnp.max(s0, axis=-1, keepdims=True))
        e1 = jnp.exp2(s1 - jnp.max(s1, axis=-1, keepdims=True))
        r0 = 1.0 / jnp.sum(e0, axis=-1, keepdims=True)
        r1 = lam / jnp.sum(e1, axis=-1, keepdims=True)
        o = _dot((e0 * r0 - e1 * r1).astype(BF16), v)
        y = o * lax.rsqrt(jnp.mean(o * o, axis=-1, keepdims=True) + RMS_EPS) * gain_ref[...]
        o_ref[...] = (y * (1.0 - lam_init)).astype(BF16)

    pl.when(t == 0)(lambda: attend(ctx))
    pl.when(t > 0)(lambda: attend(k_ref.shape[0]))


def _attn_c(lam_vecs, gain, q, k, v, ctx, lam_init):
    bsz, t_len, _ = q.shape
    qo_spec = pl.BlockSpec((None, TOKEN_BLOCK, LANES), lambda b, h, t: (b, t, h))
    kv_spec = pl.BlockSpec((None, t_len, LANES), lambda b, h, t: (b, 0, h))
    return pl.pallas_call(
        functools.partial(_attn_c_kernel, ctx=ctx, lam_init=lam_init),
        grid=(bsz, C_HEADS, t_len // TOKEN_BLOCK),
        in_specs=[pl.BlockSpec(lam_vecs.shape, lambda b, h, t: (0, 0)),
                  pl.BlockSpec(gain.shape, lambda b, h, t: (0, 0)),
                  qo_spec, kv_spec, kv_spec],
        out_specs=qo_spec,
        out_shape=jax.ShapeDtypeStruct(q.shape, BF16),
        compiler_params=_params(3),
        name="attn_differential",
    )(lam_vecs, gain, q, k, v)


def _store_token_tiles(ref, x):
    rows, d = x.shape
    rpt = d // LANES
    for c in range(rpt):
        ref[pl.ds(c, rows, stride=rpt), :] = x[:, c * LANES:(c + 1) * LANES]


def _load_token_tiles(ref, rows):
    rpt = ref.shape[0] // rows
    return jnp.concatenate([ref[pl.ds(c, rows, stride=rpt), :] for c in range(rpt)], axis=1)


def _layer_norm(z, gain, bias):
    mu = jnp.mean(z, axis=-1, keepdims=True)
    zc = z - mu
    var = jnp.mean(zc * zc, axis=-1, keepdims=True)
    return zc * lax.rsqrt(var + LN_EPS) * gain + bias


def _oproj_router_kernel(o_ref, wo_ref, x_ref, mod_ref, lng_ref, lnb_ref, wr_ref, br_ref,
                         xo_ref, h2_ref, ri_ref, wcol_ref, cnt_ref, carry_ref, *, alpha):
    @pl.when(jnp.logical_and(pl.program_id(0) == 0, pl.program_id(1) == 0))
    def _():
        carry_ref[...] = jnp.zeros_like(carry_ref)

    tb = x_ref.shape[0]
    y = _dot(o_ref[...], wo_ref[...])
    xn = _layer_norm(alpha * x_ref[...] + mod_ref[2:3, :] * y, lng_ref[0:1, :], lnb_ref[0:1, :])
    xo_ref[...] = xn
    h2 = xn * (1.0 + mod_ref[4:5, :]) + mod_ref[3:4, :]
    h_hi, h_lo = _split_bf16(h2)
    _store_token_tiles(h2_ref, h_hi.astype(F32))
    w_hi, w_lo = _split_bf16(wr_ref[...])
    logits = _nt_dot(w_hi, h_hi) + _nt_dot(w_hi, h_lo) + _nt_dot(w_lo, h_hi) + br_ref[...]

    row8 = lax.broadcasted_iota(jnp.int32, (SUBLANES, tb), 0)
    glog = jnp.where(row8 < N_GROUPS, logits[0:SUBLANES, :], NEG_INF)
    gmax = jnp.max(glog, axis=0, keepdims=True)
    gsel = jnp.min(jnp.where(glog == gmax, row8, SUBLANES), axis=0, keepdims=True)
    gprob = 1.0 / jnp.sum(jnp.exp(glog - gmax), axis=0, keepdims=True)
    elog = jnp.zeros((SUBLANES, tb), F32)
    for g in range(N_GROUPS):
        elog = jnp.where(gsel == g, logits[(g + 1) * SUBLANES:(g + 2) * SUBLANES, :], elog)
    v1 = jnp.max(elog, axis=0, keepdims=True)
    i1 = jnp.min(jnp.where(elog == v1, row8, SUBLANES), axis=0, keepdims=True)
    elog2 = jnp.where(row8 == i1, NEG_INF, elog)
    v2 = jnp.max(elog2, axis=0, keepdims=True)
    i2 = jnp.min(jnp.where(elog2 == v2, row8, SUBLANES), axis=0, keepdims=True)
    t21 = jnp.exp(v2 - v1)
    w1 = gprob / (1.0 + t21)
    w2 = gprob * t21 / (1.0 + t21)
    e1 = gsel * EXPERTS_PER_GROUP + i1
    e2 = gsel * EXPERTS_PER_GROUP + i2

    rowe = lax.broadcasted_iota(jnp.int32, (N_EXPERTS, tb), 0)
    oh1, oh2 = rowe == e1, rowe == e2
    oh = jnp.where(jnp.logical_or(oh1, oh2), 1.0, 0.0)
    before = (lax.broadcasted_iota(jnp.int32, (tb, tb), 0) < lax.broadcasted_iota(jnp.int32, (tb, tb), 1))
    base = _dot(oh.astype(BF16), jnp.where(before, 1.0, 0.0).astype(BF16)) + carry_ref[:, 0:1]
    rank1 = jnp.sum(jnp.where(oh1, base, 0.0), axis=0, keepdims=True).astype(jnp.int32)
    rank2 = jnp.sum(jnp.where(oh2, base, 0.0), axis=0, keepdims=True).astype(jnp.int32)
    carry_ref[...] = carry_ref[...] + jnp.sum(oh, axis=1, keepdims=True)
    cnt_ref[...] = carry_ref[...]

    ri_ref[...] = jnp.where(row8 == 0, e1, jnp.where(row8 == 1, e2, jnp.where(row8 == 2, rank1,
                            jnp.where(row8 == 3, rank2, 0))))
    rowl = lax.broadcasted_iota(jnp.int32, (LANES, tb), 0)
    wcol_ref[...] = jnp.where(rowl == 0, w1, jnp.where(rowl == 1, w2, 0.0)).T


def _oproj_router(o, w_o, x, mod, lng, lnb, w_router, b_router, alpha):
    bsz, t_len, d = x.shape
    n_t = t_len // TOKEN_BLOCK
    n_blk = bsz * n_t
    tok = lambda b, t: (b, t, 0)
    blk = lambda b, t: (b * n_t + t, 0, 0)
    const2 = lambda b, t: (0, 0)
    return pl.pallas_call(
        functools.partial(_oproj_router_kernel, alpha=alpha),
        grid=(bsz, n_t),
        in_specs=[
            pl.BlockSpec((None, TOKEN_BLOCK, o.shape[-1]), tok),
            pl.BlockSpec(w_o.shape, const2),
            pl.BlockSpec((None, TOKEN_BLOCK, d), tok),
            pl.BlockSpec((None, 6, d), lambda b, t: (jnp.where(t == 0, bsz, b), 0, 0)),
            pl.BlockSpec(lng.shape, const2),
            pl.BlockSpec(lnb.shape, const2),
            pl.BlockSpec(w_router.shape, const2),
            pl.BlockSpec(b_router.shape, const2),
        ],
        out_specs=[
            pl.BlockSpec((None, TOKEN_BLOCK, d), tok),
            pl.BlockSpec((TOKEN_BLOCK * d // LANES, LANES), lambda b, t: (b * n_t + t, 0)),
            pl.BlockSpec((None, SUBLANES, TOKEN_BLOCK), blk),
            pl.BlockSpec((None, TOKEN_BLOCK, LANES), blk),
            pl.BlockSpec((N_EXPERTS, LANES), const2),
        ],
        out_shape=[
            jax.ShapeDtypeStruct((bsz, t_len, d), F32),
            jax.ShapeDtypeStruct((bsz * t_len * d // LANES, LANES), F32),
            jax.ShapeDtypeStruct((n_blk, SUBLANES, TOKEN_BLOCK), jnp.int32),
            jax.ShapeDtypeStruct((n_blk, TOKEN_BLOCK, LANES), F32),
            jax.ShapeDtypeStruct((N_EXPERTS, LANES), F32),
        ],
        scratch_shapes=[pltpu.VMEM((N_EXPERTS, LANES), F32)],
        compiler_params=_params(2),
        name="oproj_ln_router",
    )(o, w_o, x, mod, lng, lnb, w_router, b_router)


def _for_each_row(n_rows, fn):
    def chunk(c, carry):
        for u in range(ROW_UNROLL):
            fn(c * ROW_UNROLL + u)
        return carry

    lax.fori_loop(0, n_rows // ROW_UNROLL, chunk, 0)


def _expert_row(dest_ref, block, k, r):
    return dest_ref[(block * 2 + k) * TOKEN_BLOCK + r]


def _token_rows(ref, row, n, rpt):
    return ref.at[pl.ds(pl.multiple_of(row * rpt, rpt), n * rpt)]


def _dispatch_kernel(off_ref, tiles_ref, dest_ref, h_ref, xs_ref, zeros, sems, *, rpt):
    tb = h_ref.shape[0] // rpt
    block = pl.program_id(0)

    @pl.when(pl.program_id(0) == 0)
    def _():
        zeros[...] = jnp.zeros_like(zeros)

        def last_tile(e):
            return pltpu.make_async_copy(
                zeros, _token_rows(xs_ref, off_ref[e] + (tiles_ref[e] - 1) * EXPERT_TILE, EXPERT_TILE, rpt), sems.at[2])

        for e in range(N_EXPERTS):
            pl.when(tiles_ref[e] > 0)(lambda e=e: last_tile(e).start())
        for e in range(N_EXPERTS):
            pl.when(tiles_ref[e] > 0)(lambda e=e: last_tile(e).wait())

        def spare_tile(i):
            return pltpu.make_async_copy(zeros, _token_rows(xs_ref, i * EXPERT_TILE, EXPERT_TILE, rpt), sems.at[2])

        first_spare = off_ref[N_EXPERTS - 1] // EXPERT_TILE + tiles_ref[N_EXPERTS - 1]
        n_tiles = xs_ref.shape[0] // (EXPERT_TILE * rpt)
        lax.fori_loop(first_spare, n_tiles, lambda i, c: (spare_tile(i).start(), c)[1], 0)
        lax.fori_loop(first_spare, n_tiles, lambda i, c: (spare_tile(i).wait(), c)[1], 0)

    def start(r):
        for k in range(2):
            pltpu.make_async_copy(_token_rows(h_ref, r, 1, rpt),
                                  _token_rows(xs_ref, _expert_row(dest_ref, block, k, r), 1, rpt),
                                  sems.at[k]).start(priority=k)

    _for_each_row(tb, start)
    for k in range(2):
        pltpu.make_async_copy(h_ref, _token_rows(xs_ref, 0, tb, rpt), sems.at[k]).wait()


def _dispatch(row_off, tiles, dest, h2, n_rows, rpt):
    n_blk = h2.shape[0] // (TOKEN_BLOCK * rpt)
    return pl.pallas_call(
        functools.partial(_dispatch_kernel, rpt=rpt),
        grid_spec=pltpu.PrefetchScalarGridSpec(
            num_scalar_prefetch=3,
            grid=(n_blk,),
            in_specs=[pl.BlockSpec((TOKEN_BLOCK * rpt, LANES), lambda i, off, nt, dst: (i, 0))],
            out_specs=pl.BlockSpec(memory_space=pl.ANY),
            scratch_shapes=[pltpu.VMEM((EXPERT_TILE * rpt, LANES), F32), pltpu.SemaphoreType.DMA((3,))],
        ),
        out_shape=jax.ShapeDtypeStruct((n_rows * rpt, LANES), F32),
        compiler_params=_params(1),
        name="moe_dispatch",
    )(row_off, tiles, dest, h2)


def _experts_kernel(te_ref, nu_ref, nxt_ref, xs_ref, wg_hbm, wu_hbm, wd_hbm, ys_ref,
                    wg_f, wu_f, wd_f, wg_s, wu_s, wd_s, slot_ref, sems, *, layer):
    t = pl.program_id(0)

    def fetch(e, slot):
        g, i = e // EXPERTS_PER_GROUP, e % EXPERTS_PER_GROUP
        return [pltpu.make_async_copy(w.at[layer, g, i], buf.at[slot], sems.at[slot, n])
                for n, (w, buf) in enumerate(((wg_hbm, wg_f), (wu_hbm, wu_f), (wd_hbm, wd_f)))]

    @pl.when(t < nu_ref[0])
    def _():
        e = te_ref[t]

        @pl.when(t == 0)
        def _():
            slot_ref[0] = 0
            for c in fetch(e, 0):
                c.start()

        @pl.when(jnp.logical_or(t == 0, e != te_ref[jnp.maximum(t - 1, 0)]))
        def _():
            slot = slot_ref[0]
            for c in fetch(e, slot):
                c.wait()
            nxt = nxt_ref[e]

            @pl.when(nxt >= 0)
            def _():
                for c in fetch(nxt, 1 - slot):
                    c.start()

            wg_s[...] = wg_f[slot].astype(BF16)
            wu_s[...] = wu_f[slot].astype(BF16)
            wd_s[...] = wd_f[slot].astype(BF16)
            slot_ref[0] = 1 - slot

        x = _load_token_tiles(xs_ref, EXPERT_TILE).astype(BF16)
        a = _dot(x, wg_s[...])
        u = _dot(x, wu_s[...])
        hid = a * (1.0 / (1.0 + jnp.exp(-a))) * u
        _store_token_tiles(ys_ref, _dot(hid.astype(BF16), wd_s[...]))

    @pl.when(t >= nu_ref[0])
    def _():
        ys_ref[...] = jnp.zeros_like(ys_ref)


def _experts(tile_expert, n_used, next_expert, xs, w_gate, w_up, w_down, layer):
    d, ff = w_gate.shape[-2:]
    tile_rows = EXPERT_TILE * d // LANES
    n_tiles = xs.shape[0] // tile_rows
    hbm = pl.BlockSpec(memory_space=pl.ANY)
    return pl.pallas_call(
        functools.partial(_experts_kernel, layer=layer),
        grid_spec=pltpu.PrefetchScalarGridSpec(
            num_scalar_prefetch=3,
            grid=(n_tiles,),
            in_specs=[
                pl.BlockSpec((tile_rows, LANES), lambda t, te, nu, nx: (jnp.minimum(t, nu[0] - 1), 0)),
                hbm, hbm, hbm,
            ],
            out_specs=pl.BlockSpec((tile_rows, LANES), lambda t, te, nu, nx: (t, 0)),
            scratch_shapes=[
                pltpu.VMEM((2, d, ff), F32), pltpu.VMEM((2, d, ff), F32), pltpu.VMEM((2, ff, d), F32),
                pltpu.VMEM((d, ff), BF16), pltpu.VMEM((d, ff), BF16), pltpu.VMEM((ff, d), BF16),
                pltpu.SMEM((1,), jnp.int32), pltpu.SemaphoreType.DMA((2, 3)),
            ],
        ),
        out_shape=jax.ShapeDtypeStruct(xs.shape, F32),
        compiler_params=_params(1),
        name="moe_experts",
    )(tile_expert, n_used, next_expert, xs, w_gate, w_up, w_down)


def _combine_kernel(dest_ref, ys_ref, wcol_ref, x_ref, mod_ref, lng_ref, lnb_ref, o_ref,
                    buf, sems, *, alpha, blocks_per_sample, first_block):
    tb, d = x_ref.shape
    rpt = d // LANES
    n_out = pl.num_programs(1)
    step = pl.program_id(0) * n_out + pl.program_id(1)
    n_steps = pl.num_programs(0) * n_out
    slot = step % 2

    def gather(s, into):
        block = lax.div(s, n_out) * blocks_per_sample + lax.rem(s, n_out) + first_block

        def start(r):
            for k in range(2):
                pltpu.make_async_copy(_token_rows(ys_ref, _expert_row(dest_ref, block, k, r), 1, rpt),
                                      _token_rows(buf.at[into, k], r, 1, rpt), sems.at[into, k]).start(priority=k)

        _for_each_row(tb, start)

    pl.when(step == 0)(lambda: gather(step, 0))
    pl.when(step + 1 < n_steps)(lambda: gather(step + 1, 1 - slot))
    for k in range(2):
        pltpu.make_async_copy(_token_rows(ys_ref, 0, tb, rpt), buf.at[slot, k], sems.at[slot, k]).wait()
    f = (wcol_ref[:, 0:1] * _load_token_tiles(buf.at[slot, 0], tb)
         + wcol_ref[:, 1:2] * _load_token_tiles(buf.at[slot, 1], tb))
    o_ref[...] = _layer_norm(alpha * x_ref[...] + mod_ref[5:6, :] * f, lng_ref[1:2, :], lnb_ref[1:2, :])


def _combine(dest, ys, wcol, x, mod, lng, lnb, alpha, first_block):
    bsz, t_len, d = x.shape
    n_t = t_len // TOKEN_BLOCK
    n_out = n_t - first_block
    tok = lambda b, t, off: (b, t + first_block, 0)
    blk = lambda b, t, off: (b * n_t + t + first_block, 0, 0)
    const2 = lambda b, t, off: (0, 0)
    return pl.pallas_call(
        functools.partial(_combine_kernel, alpha=alpha, blocks_per_sample=n_t, first_block=first_block),
        grid_spec=pltpu.PrefetchScalarGridSpec(
            num_scalar_prefetch=1,
            grid=(bsz, n_out),
            in_specs=[
                pl.BlockSpec(memory_space=pl.ANY),
                pl.BlockSpec((None, TOKEN_BLOCK, LANES), blk),
                pl.BlockSpec((None, TOKEN_BLOCK, d), tok),
                pl.BlockSpec((None, 6, d), lambda b, t, off: (jnp.where(t + first_block == 0, bsz, b), 0, 0)),
                pl.BlockSpec(lng.shape, const2),
                pl.BlockSpec(lnb.shape, const2),
            ],
            out_specs=pl.BlockSpec((None, TOKEN_BLOCK, d), lambda b, t, off: (b, t, 0)),
            scratch_shapes=[pltpu.VMEM((2, 2, TOKEN_BLOCK * d // LANES, LANES), F32),
                            pltpu.SemaphoreType.DMA((2, 2))],
        ),
        out_shape=jax.ShapeDtypeStruct((bsz, n_out * TOKEN_BLOCK, d), F32),
        compiler_params=_params(2),
        name="moe_combine_ln",
    )(dest, ys, wcol, x, mod, lng, lnb)


def _qkv_columns(kind):
    if kind == 0:
        perm, *_ = _slot_layout(A_HEAD_DIM // 4)
        nq, nk = A_HEADS, A_KV_HEADS
        q = np.concatenate([s * LANES + perm for s in range(nq)])
        k = np.concatenate([nq * LANES + s * LANES + perm for s in range(nk)])
        v = np.arange((nq + nk) * LANES, (nq + 2 * nk) * LANES)
        return np.concatenate([q, k, v]), nq, nk, nk
    perm, axis_of, freq_of, half_of, _ = _slot_layout(B_HEAD_DIM // 4)
    nf = B_HEAD_DIM // 4
    if kind == 1:
        nq = B_HEADS // 2
        q = np.concatenate([s * LANES + perm for s in range(nq)])
        k_base = B_HEADS * B_HEAD_DIM
        v_base = k_base + B_KV_HEADS * B_HEAD_DIM
        in_head = axis_of * 2 * nf + half_of * nf + freq_of
        k = np.concatenate([k_base + h * B_HEAD_DIM + in_head for h in range(B_KV_HEADS)])
        v = np.concatenate([v_base + h * B_HEAD_DIM + np.arange(LANES) % B_HEAD_DIM for h in range(B_KV_HEADS)])
        return np.concatenate([q, k, v]), nq, B_KV_HEADS, B_KV_HEADS
    nq = C_HEADS
    q = np.concatenate([s * LANES + perm for s in range(nq)])
    k = np.concatenate([nq * LANES + s * LANES + perm for s in range(nq)])
    v = np.arange(2 * nq * LANES, 3 * nq * LANES)
    return np.concatenate([q, k, v]), nq, nq, nq


def kernel(x, c, ctx, c_ctx, w_mod, b_mod, ln_gain, ln_bias, a_w_qkv, a_q_gain, a_k_gain, a_w_o, b_w_qkv, b_sink, b_w_o, c_w_qkv, c_lam_q1, c_lam_k1, c_lam_q2, c_lam_k2, c_subln_gain, c_w_o, moe_w_group, moe_b_group, moe_w_expert, moe_b_expert, moe_w_gate, moe_w_up, moe_w_down):
    bsz, seq, d = x.shape
    n_ctx = ctx.shape[1]
    depth = w_mod.shape[0]
    assert n_ctx == TOKEN_BLOCK and seq % TOKEN_BLOCK == 0 and seq >= TOKEN_BLOCK + 2 * WINDOW
    assert d % (SUBLANES * LANES) == 0
    t_len = n_ctx + seq
    n_tok = bsz * t_len
    alpha = (2.0 * depth) ** 0.25

    mod_rows = -(-(bsz + 1) // SUBLANES) * SUBLANES
    cc = jnp.concatenate([c, c_ctx[None, :], jnp.zeros((mod_rows - bsz - 1, d), F32)], axis=0)
    mod_all = _modulation(cc, w_mod, b_mod).reshape(depth, mod_rows, 6, d)

    xs_rows = 2 * n_tok + N_EXPERTS * EXPERT_TILE
    xs_rows = -(-xs_rows // EXPERT_TILE) * EXPERT_TILE
    n_tiles = xs_rows // EXPERT_TILE

    stream = jnp.concatenate([ctx, x], axis=1)
    tables = {hd: _rope_tables(seq, n_ctx, hd // 4, hd ** -0.5 * LOG2_E) for hd in (A_HEAD_DIM, B_HEAD_DIM)}
    perm_a = _slot_layout(A_HEAD_DIM // 4)[0]

    for i in range(depth):
        kind, j = i % N_MIXERS, i // N_MIXERS
        mod = mod_all[i]
        cols, n_q, n_k, n_v = _qkv_columns(kind)
        if kind == 0:
            w = a_w_qkv[j][:, cols].astype(BF16)
            gains = (a_q_gain[j][perm_a][None, :], a_k_gain[j][perm_a][None, :])
            q, k, v = _qkv_call(stream, mod, w, tables[A_HEAD_DIM], gains, n_q, n_k, n_v)
            o = _attn_a(q, k, v, n_ctx)
            w_o = a_w_o[j]
        elif kind == 1:
            w = b_w_qkv[j][:, cols].astype(BF16)
            q, k, v = _qkv_call(stream, mod, w, tables[B_HEAD_DIM], None, n_q, n_k, n_v)
            o = _attn_b(b_sink[j], q, k, v, n_ctx)
            w_o = b_w_o[j]
        else:
            w = c_w_qkv[j][:, cols].astype(BF16)
            q, k, v = _qkv_call(stream, mod, w, tables[C_HEAD_DIM], None, n_q, n_k, n_v)
            lam_init = 0.8 - 0.6 * float(np.exp(-0.3 * i))
            lam_vecs = jnp.stack([c_lam_q1[j], c_lam_k1[j], c_lam_q2[j], c_lam_k2[j]], axis=0)
            o = _attn_c(lam_vecs, c_subln_gain[j][None, :], q, k, v, n_ctx, lam_init)
            w_o = c_w_o[j]

        w_router = jnp.zeros((ROUTER_ROWS, d), F32)
        w_router = w_router.at[0:N_GROUPS].set(moe_w_group[i].T).at[SUBLANES:SUBLANES + N_EXPERTS].set(moe_w_expert[i].T)
        b_router = jnp.zeros((ROUTER_ROWS, 1), F32)
        b_router = b_router.at[0:N_GROUPS, 0].set(moe_b_group[i]).at[SUBLANES:SUBLANES + N_EXPERTS, 0].set(moe_b_expert[i])

        stream, h2, rinfo, wcol, counts = _oproj_router(
            o, w_o.astype(BF16), stream, mod, ln_gain[i], ln_bias[i], w_router, b_router, alpha)

        cnt = counts[:, 0].astype(jnp.int32)
        tiles = (cnt + EXPERT_TILE - 1) // EXPERT_TILE
        tile_end = jnp.cumsum(tiles)
        row_off = (tile_end - tiles) * EXPERT_TILE
        n_used = tile_end[-1:].astype(jnp.int32)
        tile_ids = jnp.arange(n_tiles, dtype=jnp.int32)
        tile_expert = jnp.minimum(
            jnp.sum((tile_end[None, :] <= tile_ids[:, None]).astype(jnp.int32), axis=1), N_EXPERTS - 1)

        ids = jnp.arange(N_EXPERTS, dtype=jnp.int32)
        later_used = jnp.logical_and(ids[None, :] > ids[:, None], tiles[None, :] > 0)
        next_expert = jnp.min(jnp.where(later_used, ids[None, :], N_EXPERTS), axis=1)
        next_expert = jnp.where(next_expert == N_EXPERTS, -1, next_expert).astype(jnp.int32)

        expert_start = jnp.sum(jnp.where(rinfo[:, 0:2, :, None] == ids, row_off, 0), axis=-1)
        dest = (expert_start + rinfo[:, 2:4, :]).reshape(-1)

        xs = _dispatch(row_off, tiles, dest, h2, xs_rows, d // LANES)
        ys = _experts(tile_expert, n_used, next_expert, xs, moe_w_gate, moe_w_up, moe_w_down, i)
        last = i == depth - 1
        stream = _combine(dest, ys, wcol, stream, mod, ln_gain[i], ln_bias[i], alpha, 1 if last else 0)

    return stream
```

```python
import functools

import numpy as np
import jax
import jax.numpy as jnp
from jax import lax
from jax.experimental import pallas as pl
from jax.experimental.pallas import tpu as pltpu

GRID_W = 64
ROPE_THETA = 10000.0
LN_EPS = 1e-5
RMS_EPS = 1e-6
NEG_INF = -1e30
N_MIXERS = 3
LOG2_E = float(np.log2(np.e))

A_HEADS, A_KV_HEADS, A_HEAD_DIM = 8, 2, 128
B_HEADS, B_KV_HEADS, B_HEAD_DIM = 16, 4, 64
WINDOW = 128
C_HEADS, C_HEAD_DIM = 8, 64
N_GROUPS, EXPERTS_PER_GROUP = 4, 8
N_EXPERTS = N_GROUPS * EXPERTS_PER_GROUP

LANES = 128
SUBLANES = 8
HALF = LANES // 2
TOKEN_BLOCK = 256
EXPERT_TILE = 256
ROUTER_ROWS = 128
ROW_UNROLL = 8
C_HEADS_PER_STEP = 4
VMEM_LIMIT = 56 * 1024 * 1024

F32 = jnp.float32
BF16 = jnp.bfloat16


def _nt_dot(a, b):
    return lax.dot_general(a, b, (((1,), (1,)), ((), ())), preferred_element_type=F32)


def _dot(a, b):
    return jnp.dot(a, b, preferred_element_type=F32)


def _split_bf16(x):
    hi = x.astype(BF16)
    lo = (x - hi.astype(F32)).astype(BF16)
    return hi, lo


def _params(n_axes):
    return pltpu.CompilerParams(dimension_semantics=("arbitrary",) * n_axes, vmem_limit_bytes=VMEM_LIMIT)


def _slot_layout(n_freq):
    n_sub = LANES // (4 * n_freq)
    perm = np.zeros(LANES, np.int32)
    axis_of = np.zeros(LANES, np.int32)
    freq_of = np.zeros(LANES, np.int32)
    half_of = np.zeros(LANES, np.int32)
    sub_of = np.zeros(LANES, np.int32)
    for half in range(2):
        for sub in range(n_sub):
            for axis in range(2):
                for f in range(n_freq):
                    l = half * HALF + sub * 2 * n_freq + axis * n_freq + f
                    perm[l] = sub * 4 * n_freq + axis * 2 * n_freq + half * n_freq + f
                    axis_of[l], freq_of[l], half_of[l], sub_of[l] = axis, f, half, sub
    return perm, axis_of, freq_of, half_of, sub_of


def _rope_tables(seq, ctx, n_freq, q_scale):
    _, axis_of, freq_of, half_of, _ = _slot_layout(n_freq)
    rows = seq // GRID_W
    row = jnp.broadcast_to(jnp.arange(rows, dtype=F32)[:, None], (rows, GRID_W)).reshape(-1)
    col = jnp.broadcast_to(jnp.arange(GRID_W, dtype=F32)[None, :], (rows, GRID_W)).reshape(-1)
    inv = ROPE_THETA ** (-jnp.arange(n_freq, dtype=F32) / n_freq)
    ang = jnp.stack([row[:, None] * inv, col[:, None] * inv], axis=1)
    cos, sin = jnp.cos(ang), jnp.sin(ang)
    cos_l = cos[:, axis_of, freq_of]
    sin_l = sin[:, axis_of, freq_of] * jnp.asarray(np.where(half_of == 0, -1.0, 1.0), F32)
    cos_l = jnp.concatenate([jnp.ones((ctx, LANES), F32), cos_l], axis=0)
    sin_l = jnp.concatenate([jnp.zeros((ctx, LANES), F32), sin_l], axis=0)
    return cos_l * q_scale, sin_l * q_scale, cos_l, sin_l


def _mod_kernel(c_ref, w_ref, b_ref, o_ref):
    c = c_ref[...]
    s = c * (1.0 / (1.0 + jnp.exp(-c)))
    s_hi, s_lo = _split_bf16(s)
    w_hi, w_lo = _split_bf16(w_ref[...])
    o_ref[...] = _dot(s_hi, w_hi) + _dot(s_lo, w_hi) + _dot(s_hi, w_lo) + b_ref[...]


def _modulation(cc, w_mod, b_mod):
    depth, d, n = w_mod.shape
    rows = cc.shape[0]
    tn = n // 4
    return pl.pallas_call(
        _mod_kernel,
        grid=(depth, n // tn),
        in_specs=[
            pl.BlockSpec((rows, d), lambda l, j: (0, 0)),
            pl.BlockSpec((None, d, tn), lambda l, j: (l, 0, j)),
            pl.BlockSpec((None, 1, tn), lambda l, j: (l, 0, j)),
        ],
        out_specs=pl.BlockSpec((None, rows, tn), lambda l, j: (l, 0, j)),
        out_shape=jax.ShapeDtypeStruct((depth, rows, n), F32),
        compiler_params=_params(2),
        name="modulation",
    )(cc, w_mod, b_mod.reshape(depth, 1, n))


def _qkv_kernel(*refs, n_q, n_k, rms):
    if rms:
        x_ref, mod_ref, w_ref, cq_ref, sq_ref, ck_ref, sk_ref, gq_ref, gk_ref, q_ref, k_ref, v_ref = refs
    else:
        x_ref, mod_ref, w_ref, cq_ref, sq_ref, ck_ref, sk_ref, q_ref, k_ref, v_ref = refs
        gq_ref = gk_ref = None
    sh, sc = mod_ref[0:1, :], mod_ref[1:2, :]
    h = (x_ref[...] * (1.0 + sc) + sh).astype(BF16)

    def rope_slots(y, cos, sin, gain_ref, out_ref, n_slots):
        for j in range(n_slots):
            yj = y[:, j * LANES:(j + 1) * LANES]
            if gain_ref is not None:
                ms = jnp.mean(yj * yj, axis=-1, keepdims=True)
                yj = yj * lax.rsqrt(ms + RMS_EPS) * gain_ref[...]
            out = yj * cos + pltpu.roll(yj, HALF, 1) * sin
            out_ref[:, j * LANES:(j + 1) * LANES] = out.astype(BF16)

    nq, nk = n_q * LANES, n_k * LANES
    rope_slots(_dot(h, w_ref[:, 0:nq]), cq_ref[...], sq_ref[...], gq_ref, q_ref, n_q)
    rope_slots(_dot(h, w_ref[:, nq:nq + nk]), ck_ref[...], sk_ref[...], gk_ref, k_ref, n_k)
    v_ref[...] = _dot(h, w_ref[:, nq + nk:]).astype(BF16)


def _qkv_call(x, mod, w, tables, gains, n_q, n_k, n_v):
    bsz, t_len, d = x.shape
    n_t = t_len // TOKEN_BLOCK
    n_rows = mod.shape[0]
    rms = gains is not None
    tok = lambda b, t: (b, t, 0)
    tab_spec = pl.BlockSpec((TOKEN_BLOCK, LANES), lambda b, t: (t, 0))
    in_specs = [
        pl.BlockSpec((None, TOKEN_BLOCK, d), tok),
        pl.BlockSpec((None, 6, d), lambda b, t: (jnp.where(t == 0, bsz, b), 0, 0)),
        pl.BlockSpec(w.shape, lambda b, t: (0, 0)),
        tab_spec, tab_spec, tab_spec, tab_spec,
    ]
    args = [x, mod, w, *tables]
    if rms:
        in_specs += [pl.BlockSpec((1, LANES), lambda b, t: (0, 0))] * 2
        args += list(gains)
    del n_rows
    return pl.pallas_call(
        functools.partial(_qkv_kernel, n_q=n_q, n_k=n_k, rms=rms),
        grid=(bsz, n_t),
        in_specs=in_specs,
        out_specs=[pl.BlockSpec((None, TOKEN_BLOCK, n * LANES), tok) for n in (n_q, n_k, n_v)],
        out_shape=[jax.ShapeDtypeStruct((bsz, t_len, n * LANES), BF16) for n in (n_q, n_k, n_v)],
        compiler_params=_params(2),
        name="qkv_rope",
    )(*args)


def _softmax_pv(s, v_ext):
    e = jnp.exp2(s - jnp.max(s, axis=-1, keepdims=True))
    oe = _dot(e.astype(BF16), v_ext)
    return oe[:, :LANES], oe[:, LANES:]


def _fill_v_ext(v_ref, v_ext):
    v_ext[:, :LANES] = v_ref[...]
    v_ext[:, LANES:] = jnp.ones((v_ref.shape[0], LANES), BF16)


def _attn_a_kernel(q_ref, k_ref, v_ref, o_ref, v_ext, *, ctx, group):
    t = pl.program_id(2)
    pl.when(t == 0)(lambda: _fill_v_ext(v_ref, v_ext))

    def attend(n_keys):
        k, v = k_ref[0:n_keys, :], v_ext[0:n_keys, :]
        for g in range(group):
            num, den = _softmax_pv(_nt_dot(q_ref[:, g * LANES:(g + 1) * LANES], k), v)
            o_ref[:, g * LANES:(g + 1) * LANES] = (num / den).astype(BF16)

    pl.when(t == 0)(lambda: attend(ctx))
    pl.when(t > 0)(lambda: attend(k_ref.shape[0]))


def _attn_a(q, k, v, ctx):
    bsz, t_len, _ = q.shape
    group = A_HEADS // A_KV_HEADS
    qo_spec = pl.BlockSpec((None, TOKEN_BLOCK, group * LANES), lambda b, h, t: (b, t, h))
    kv_spec = pl.BlockSpec((None, t_len, LANES), lambda b, h, t: (b, 0, h))
    return pl.pallas_call(
        functools.partial(_attn_a_kernel, ctx=ctx, group=group),
        grid=(bsz, A_KV_HEADS, t_len // TOKEN_BLOCK),
        in_specs=[qo_spec, kv_spec, kv_spec],
        out_specs=qo_spec,
        out_shape=jax.ShapeDtypeStruct(q.shape, BF16),
        scratch_shapes=[pltpu.VMEM((t_len, 2 * LANES), BF16)],
        compiler_params=_params(3),
        name="attn_dense_gqa",
    )(q, k, v)


def _attn_b_kernel(sink_ref, q_ref, k_ref, v_ref, o_ref, v_ext, *, ctx, seq, group):
    h, t = pl.program_id(1), pl.program_id(2)
    lane = lax.broadcasted_iota(jnp.int32, (1, LANES), 1)
    first_head = (lane // (HALF // 2)) % 2 == 0
    lower = lane < HALF
    span = TOKEN_BLOCK + 2 * WINDOW

    @pl.when(t == 0)
    def _():
        v_ext[...] = jnp.where(lower, v_ref[...], jnp.ones_like(v_ref))

    def run(kk, vv, valid):
        for sl in range(group // 2):
            qs = q_ref[:, sl * LANES:(sl + 1) * LANES]
            outs = []
            for j in range(2):
                qm = jnp.where(first_head if j == 0 else jnp.logical_not(first_head), qs, jnp.zeros_like(qs))
                s = _nt_dot(qm, kk)
                if valid is not None:
                    s = jnp.concatenate([s[:, :ctx], jnp.where(valid, s[:, ctx:], NEG_INF)], axis=1)
                snk = sink_ref[h * group + sl * 2 + j] * LOG2_E
                m = jnp.maximum(jnp.max(s, axis=-1, keepdims=True), snk)
                oe = _dot(jnp.exp2(s - m).astype(BF16), vv)
                outs.append((oe, pltpu.roll(oe, HALF, 1), jnp.exp2(snk - m)))
            (oa, ra, za), (ob, rb, zb) = outs
            num = jnp.where(lower, oa, rb)
            den = jnp.where(lower, ra + za, ob + zb)
            o_ref[:, sl * LANES:(sl + 1) * LANES] = (num / den).astype(BF16)

    @pl.when(t == 0)
    def _():
        run(k_ref[0:ctx, :], v_ext[0:ctx, :], None)

    @pl.when(t > 0)
    def _():
        p0 = (t - 1) * TOKEN_BLOCK
        start = pl.multiple_of(jnp.clip(p0 - WINDOW, 0, seq - span), WINDOW)
        kk = jnp.concatenate([k_ref[0:ctx, :], k_ref[pl.ds(ctx + start, span), :]], axis=0)
        vv = jnp.concatenate([v_ext[0:ctx, :], v_ext[pl.ds(ctx + start, span), :]], axis=0)
        dist = (lax.broadcasted_iota(jnp.int32, (TOKEN_BLOCK, span), 1)
                - lax.broadcasted_iota(jnp.int32, (TOKEN_BLOCK, span), 0) + (start - p0))
        run(kk, vv, jnp.abs(dist) <= WINDOW)


def _attn_b(sink, q, k, v, ctx):
    bsz, t_len, _ = q.shape
    group = B_HEADS // B_KV_HEADS
    width = group // 2 * LANES
    qo_spec = pl.BlockSpec((None, TOKEN_BLOCK, width), lambda b, h, t, s: (b, t, h))
    kv_spec = pl.BlockSpec((None, t_len, LANES), lambda b, h, t, s: (b, 0, h))
    return pl.pallas_call(
        functools.partial(_attn_b_kernel, ctx=ctx, seq=t_len - ctx, group=group),
        grid_spec=pltpu.PrefetchScalarGridSpec(
            num_scalar_prefetch=1,
            grid=(bsz, B_KV_HEADS, t_len // TOKEN_BLOCK),
            in_specs=[qo_spec, kv_spec, kv_spec],
            out_specs=qo_spec,
            scratch_shapes=[pltpu.VMEM((t_len, LANES), BF16)],
        ),
        out_shape=jax.ShapeDtypeStruct(q.shape, BF16),
        compiler_params=_params(3),
        name="attn_window_sink",
    )(sink, q, k, v)


def _attn_c_kernel(lam_ref, gain_ref, q_ref, k_ref, v_ref, o_ref, *, ctx, lam_init):
    t = pl.program_id(2)
    lane = lax.broadcasted_iota(jnp.int32, (1, LANES), 1)
    first_map = (lane // (HALF // 2)) % 2 == 0
    lam = (jnp.exp(jnp.sum(lam_ref[0:1, :] * lam_ref[1:2, :], axis=-1, keepdims=True))
           - jnp.exp(jnp.sum(lam_ref[2:3, :] * lam_ref[3:4, :], axis=-1, keepdims=True)) + lam_init)

    def attend(n_keys):
        for h in range(q_ref.shape[1] // LANES):
            head = slice(h * LANES, (h + 1) * LANES)
            k, v = k_ref[0:n_keys, head], v_ref[0:n_keys, head]
            q = q_ref[:, head]
            zero = jnp.zeros_like(q)
            s0 = _nt_dot(jnp.where(first_map, q, zero), k)
            s1 = _nt_dot(jnp.where(first_map, zero, q), k)
            e0 = jnp.exp2(s0 - jnp.max(s0, axis=-1, keepdims=True))
            e1 = jnp.exp2(s1 - jnp.max(s1, axis=-1, keepdims=True))
            r0 = 1.0 / jnp.sum(e0, axis=-1, keepdims=True)
            r1 = lam / jnp.sum(e1, axis=-1, keepdims=True)
            o = _dot((e0 * r0 - e1 * r1).astype(BF16), v)
            y = o * lax.rsqrt(jnp.mean(o * o, axis=-1, keepdims=True) + RMS_EPS) * gain_ref[...]
            o_ref[:, head] = (y * (1.0 - lam_init)).astype(BF16)

    pl.when(t == 0)(lambda: attend(ctx))
    pl.when(t > 0)(lambda: attend(k_ref.shape[0]))


def _attn_c(lam_vecs, gain, q, k, v, ctx, lam_init):
    bsz, t_len, _ = q.shape
    width = C_HEADS_PER_STEP * LANES
    qo_spec = pl.BlockSpec((None, TOKEN_BLOCK, width), lambda b, h, t: (b, t, h))
    kv_spec = pl.BlockSpec((None, t_len, width), lambda b, h, t: (b, 0, h))
    return pl.pallas_call(
        functools.partial(_attn_c_kernel, ctx=ctx, lam_init=lam_init),
        grid=(bsz, C_HEADS // C_HEADS_PER_STEP, t_len // TOKEN_BLOCK),
        in_specs=[pl.BlockSpec(lam_vecs.shape, lambda b, h, t: (0, 0)),
                  pl.BlockSpec(gain.shape, lambda b, h, t: (0, 0)),
                  qo_spec, kv_spec, kv_spec],
        out_specs=qo_spec,
        out_shape=jax.ShapeDtypeStruct(q.shape, BF16),
        compiler_params=_params(3),
        name="attn_differential",
    )(lam_vecs, gain, q, k, v)


def _store_token_tiles(ref, x):
    rows, d = x.shape
    rpt = d // LANES
    for c in range(rpt):
        ref[pl.ds(c, rows, stride=rpt), :] = x[:, c * LANES:(c + 1) * LANES]


def _load_token_tiles(ref, rows):
    rpt = ref.shape[0] // rows
    return jnp.concatenate([ref[pl.ds(c, rows, stride=rpt), :] for c in range(rpt)], axis=1)


def _layer_norm(z, gain, bias):
    mu = jnp.mean(z, axis=-1, keepdims=True)
    zc = z - mu
    var = jnp.mean(zc * zc, axis=-1, keepdims=True)
    return zc * lax.rsqrt(var + LN_EPS) * gain + bias


def _oproj_router_kernel(o_ref, wo_ref, x_ref, mod_ref, lng_ref, lnb_ref, wr_ref, br_ref,
                         xo_ref, h2_ref, ri_ref, wcol_ref, cnt_ref, carry_ref, *, alpha):
    @pl.when(jnp.logical_and(pl.program_id(0) == 0, pl.program_id(1) == 0))
    def _():
        carry_ref[...] = jnp.zeros_like(carry_ref)

    tb = x_ref.shape[0]
    y = _dot(o_ref[...], wo_ref[...])
    xn = _layer_norm(alpha * x_ref[...] + mod_ref[2:3, :] * y, lng_ref[0:1, :], lnb_ref[0:1, :])
    xo_ref[...] = xn
    h2 = xn * (1.0 + mod_ref[4:5, :]) + mod_ref[3:4, :]
    h_hi, h_lo = _split_bf16(h2)
    _store_token_tiles(h2_ref, h_hi.astype(F32))
    w_hi, w_lo = _split_bf16(wr_ref[...])
    logits = _nt_dot(w_hi, h_hi) + _nt_dot(w_hi, h_lo) + _nt_dot(w_lo, h_hi) + br_ref[...]

    row8 = lax.broadcasted_iota(jnp.int32, (SUBLANES, tb), 0)
    glog = jnp.where(row8 < N_GROUPS, logits[0:SUBLANES, :], NEG_INF)
    gmax = jnp.max(glog, axis=0, keepdims=True)
    gsel = jnp.min(jnp.where(glog == gmax, row8, SUBLANES), axis=0, keepdims=True)
    gprob = 1.0 / jnp.sum(jnp.exp(glog - gmax), axis=0, keepdims=True)
    elog = jnp.zeros((SUBLANES, tb), F32)
    for g in range(N_GROUPS):
        elog = jnp.where(gsel == g, logits[(g + 1) * SUBLANES:(g + 2) * SUBLANES, :], elog)
    v1 = jnp.max(elog, axis=0, keepdims=True)
    i1 = jnp.min(jnp.where(elog == v1, row8, SUBLANES), axis=0, keepdims=True)
    elog2 = jnp.where(row8 == i1, NEG_INF, elog)
    v2 = jnp.max(elog2, axis=0, keepdims=True)
    i2 = jnp.min(jnp.where(elog2 == v2, row8, SUBLANES), axis=0, keepdims=True)
    t21 = jnp.exp(v2 - v1)
    w1 = gprob / (1.0 + t21)
    w2 = gprob * t21 / (1.0 + t21)
    e1 = gsel * EXPERTS_PER_GROUP + i1
    e2 = gsel * EXPERTS_PER_GROUP + i2

    rowe = lax.broadcasted_iota(jnp.int32, (N_EXPERTS, tb), 0)
    oh1, oh2 = rowe == e1, rowe == e2
    oh = jnp.where(jnp.logical_or(oh1, oh2), 1.0, 0.0)
    before = (lax.broadcasted_iota(jnp.int32, (tb, tb), 0) < lax.broadcasted_iota(jnp.int32, (tb, tb), 1))
    base = _dot(oh.astype(BF16), jnp.where(before, 1.0, 0.0).astype(BF16)) + carry_ref[:, 0:1]
    rank1 = jnp.sum(jnp.where(oh1, base, 0.0), axis=0, keepdims=True).astype(jnp.int32)
    rank2 = jnp.sum(jnp.where(oh2, base, 0.0), axis=0, keepdims=True).astype(jnp.int32)
    carry_ref[...] = carry_ref[...] + jnp.sum(oh, axis=1, keepdims=True)
    cnt_ref[...] = carry_ref[...]

    ri_ref[...] = jnp.where(row8 == 0, e1, jnp.where(row8 == 1, e2, jnp.where(row8 == 2, rank1,
                            jnp.where(row8 == 3, rank2, 0))))
    rowl = lax.broadcasted_iota(jnp.int32, (LANES, tb), 0)
    wcol_ref[...] = jnp.where(rowl == 0, w1, jnp.where(rowl == 1, w2, 0.0)).T


def _oproj_router(o, w_o, x, mod, lng, lnb, w_router, b_router, alpha, first_block):
    bsz, t_in, d = x.shape
    n_t = t_in // TOKEN_BLOCK - first_block
    t_len = n_t * TOKEN_BLOCK
    n_blk = bsz * n_t
    tok_in = lambda b, t: (b, t + first_block, 0)
    tok = lambda b, t: (b, t, 0)
    blk = lambda b, t: (b * n_t + t, 0, 0)
    const2 = lambda b, t: (0, 0)
    return pl.pallas_call(
        functools.partial(_oproj_router_kernel, alpha=alpha),
        grid=(bsz, n_t),
        in_specs=[
            pl.BlockSpec((None, TOKEN_BLOCK, o.shape[-1]), tok_in),
            pl.BlockSpec(w_o.shape, const2),
            pl.BlockSpec((None, TOKEN_BLOCK, d), tok_in),
            pl.BlockSpec((None, 6, d), lambda b, t: (jnp.where(t + first_block == 0, bsz, b), 0, 0)),
            pl.BlockSpec(lng.shape, const2),
            pl.BlockSpec(lnb.shape, const2),
            pl.BlockSpec(w_router.shape, const2),
            pl.BlockSpec(b_router.shape, const2),
        ],
        out_specs=[
            pl.BlockSpec((None, TOKEN_BLOCK, d), tok),
            pl.BlockSpec((TOKEN_BLOCK * d // LANES, LANES), lambda b, t: (b * n_t + t, 0)),
            pl.BlockSpec((None, SUBLANES, TOKEN_BLOCK), blk),
            pl.BlockSpec((None, TOKEN_BLOCK, LANES), blk),
            pl.BlockSpec((N_EXPERTS, LANES), const2),
        ],
        out_shape=[
            jax.ShapeDtypeStruct((bsz, t_len, d), F32),
            jax.ShapeDtypeStruct((bsz * t_len * d // LANES, LANES), F32),
            jax.ShapeDtypeStruct((n_blk, SUBLANES, TOKEN_BLOCK), jnp.int32),
            jax.ShapeDtypeStruct((n_blk, TOKEN_BLOCK, LANES), F32),
            jax.ShapeDtypeStruct((N_EXPERTS, LANES), F32),
        ],
        scratch_shapes=[pltpu.VMEM((N_EXPERTS, LANES), F32)],
        compiler_params=_params(2),
        name="oproj_ln_router",
    )(o, w_o, x, mod, lng, lnb, w_router, b_router)


def _for_each_row(n_rows, fn):
    def chunk(c, carry):
        for u in range(ROW_UNROLL):
            fn(c * ROW_UNROLL + u)
        return carry

    lax.fori_loop(0, n_rows // ROW_UNROLL, chunk, 0)


def _expert_row(dest_ref, block, k, r):
    return dest_ref[(block * 2 + k) * TOKEN_BLOCK + r]


def _token_rows(ref, row, n, rpt):
    return ref.at[pl.ds(pl.multiple_of(row * rpt, rpt), n * rpt)]


def _dispatch_kernel(off_ref, tiles_ref, dest_ref, h_ref, xs_ref, zeros, sems, *, rpt):
    tb = h_ref.shape[0] // rpt
    block = pl.program_id(0)

    @pl.when(pl.program_id(0) == 0)
    def _():
        zeros[...] = jnp.zeros_like(zeros)

        def last_tile(e):
            return pltpu.make_async_copy(
                zeros, _token_rows(xs_ref, off_ref[e] + (tiles_ref[e] - 1) * EXPERT_TILE, EXPERT_TILE, rpt), sems.at[2])

        for e in range(N_EXPERTS):
            pl.when(tiles_ref[e] > 0)(lambda e=e: last_tile(e).start())
        for e in range(N_EXPERTS):
            pl.when(tiles_ref[e] > 0)(lambda e=e: last_tile(e).wait())

        def spare_tile(i):
            return pltpu.make_async_copy(zeros, _token_rows(xs_ref, i * EXPERT_TILE, EXPERT_TILE, rpt), sems.at[2])

        first_spare = off_ref[N_EXPERTS - 1] // EXPERT_TILE + tiles_ref[N_EXPERTS - 1]
        n_tiles = xs_ref.shape[0] // (EXPERT_TILE * rpt)
        lax.fori_loop(first_spare, n_tiles, lambda i, c: (spare_tile(i).start(), c)[1], 0)
        lax.fori_loop(first_spare, n_tiles, lambda i, c: (spare_tile(i).wait(), c)[1], 0)

    for sub in range(tb // TOKEN_BLOCK):
        def start(r, sub=sub):
            for k in range(2):
                pltpu.make_async_copy(
                    _token_rows(h_ref, sub * TOKEN_BLOCK + r, 1, rpt),
                    _token_rows(xs_ref, _expert_row(dest_ref, block * (tb // TOKEN_BLOCK) + sub, k, r), 1, rpt),
                    sems.at[k]).start(priority=k)

        _for_each_row(TOKEN_BLOCK, start)
    for k in range(2):
        pltpu.make_async_copy(h_ref, _token_rows(xs_ref, 0, tb, rpt), sems.at[k]).wait()


def _dispatch(row_off, tiles, dest, h2, n_rows, rpt):
    n_blk = h2.shape[0] // (TOKEN_BLOCK * rpt)
    per_step = 1
    return pl.pallas_call(
        functools.partial(_dispatch_kernel, rpt=rpt),
        grid_spec=pltpu.PrefetchScalarGridSpec(
            num_scalar_prefetch=3,
            grid=(n_blk // per_step,),
            in_specs=[pl.BlockSpec((per_step * TOKEN_BLOCK * rpt, LANES), lambda i, off, nt, dst: (i, 0))],
            out_specs=pl.BlockSpec(memory_space=pl.ANY),
            scratch_shapes=[pltpu.VMEM((EXPERT_TILE * rpt, LANES), F32), pltpu.SemaphoreType.DMA((3,))],
        ),
        out_shape=jax.ShapeDtypeStruct((n_rows * rpt, LANES), F32),
        compiler_params=_params(1),
        name="moe_dispatch",
    )(row_off, tiles, dest, h2)


def _experts_kernel(te_ref, nu_ref, nxt_ref, xs_ref, wg_hbm, wu_hbm, wd_hbm, ys_ref,
                    wg_f, wu_f, wd_f, wg_s, wu_s, wd_s, slot_ref, sems, *, layer):
    t = pl.program_id(0)

    def fetch(e, slot):
        g, i = e // EXPERTS_PER_GROUP, e % EXPERTS_PER_GROUP
        return [pltpu.make_async_copy(w.at[layer, g, i], buf.at[slot], sems.at[slot, n])
                for n, (w, buf) in enumerate(((wg_hbm, wg_f), (wu_hbm, wu_f), (wd_hbm, wd_f)))]

    @pl.when(t < nu_ref[0])
    def _():
        e = te_ref[t]

        @pl.when(t == 0)
        def _():
            slot_ref[0] = 0
            for c in fetch(e, 0):
                c.start()

        @pl.when(jnp.logical_or(t == 0, e != te_ref[jnp.maximum(t - 1, 0)]))
        def _():
            slot = slot_ref[0]
            for c in fetch(e, slot):
                c.wait()
            nxt = nxt_ref[e]

            @pl.when(nxt >= 0)
            def _():
                for c in fetch(nxt, 1 - slot):
                    c.start()

            wg_s[...] = wg_f[slot].astype(BF16)
            wu_s[...] = wu_f[slot].astype(BF16)
            wd_s[...] = wd_f[slot].astype(BF16)
            slot_ref[0] = 1 - slot

        x = _load_token_tiles(xs_ref, EXPERT_TILE).astype(BF16)
        a = _dot(x, wg_s[...])
        u = _dot(x, wu_s[...])
        hid = a * (1.0 / (1.0 + jnp.exp(-a))) * u
        _store_token_tiles(ys_ref, _dot(hid.astype(BF16), wd_s[...]))

    @pl.when(t >= nu_ref[0])
    def _():
        ys_ref[...] = jnp.zeros_like(ys_ref)


def _experts(tile_expert, n_used, next_expert, xs, w_gate, w_up, w_down, layer):
    d, ff = w_gate.shape[-2:]
    tile_rows = EXPERT_TILE * d // LANES
    n_tiles = xs.shape[0] // tile_rows
    hbm = pl.BlockSpec(memory_space=pl.ANY)
    return pl.pallas_call(
        functools.partial(_experts_kernel, layer=layer),
        grid_spec=pltpu.PrefetchScalarGridSpec(
            num_scalar_prefetch=3,
            grid=(n_tiles,),
            in_specs=[
                pl.BlockSpec((tile_rows, LANES), lambda t, te, nu, nx: (jnp.minimum(t, nu[0] - 1), 0)),
                hbm, hbm, hbm,
            ],
            out_specs=pl.BlockSpec((tile_rows, LANES), lambda t, te, nu, nx: (t, 0)),
            scratch_shapes=[
                pltpu.VMEM((2, d, ff), F32), pltpu.VMEM((2, d, ff), F32), pltpu.VMEM((2, ff, d), F32),
                pltpu.VMEM((d, ff), BF16), pltpu.VMEM((d, ff), BF16), pltpu.VMEM((ff, d), BF16),
                pltpu.SMEM((1,), jnp.int32), pltpu.SemaphoreType.DMA((2, 3)),
            ],
        ),
        out_shape=jax.ShapeDtypeStruct(xs.shape, F32),
        compiler_params=_params(1),
        name="moe_experts",
    )(tile_expert, n_used, next_expert, xs, w_gate, w_up, w_down)


def _combine_kernel(dest_ref, ys_ref, wcol_ref, x_ref, mod_ref, lng_ref, lnb_ref, o_ref, buf, sems, *, alpha):
    tb, d = x_ref.shape
    rpt = d // LANES
    step = pl.program_id(0) * pl.num_programs(1) + pl.program_id(1)
    n_steps = pl.num_programs(0) * pl.num_programs(1)
    slot = step % 2

    def gather(block, into):
        def start(r):
            for k in range(2):
                pltpu.make_async_copy(_token_rows(ys_ref, _expert_row(dest_ref, block, k, r), 1, rpt),
                                      _token_rows(buf.at[into, k], r, 1, rpt), sems.at[into, k]).start(priority=k)

        _for_each_row(tb, start)

    pl.when(step == 0)(lambda: gather(step, 0))
    pl.when(step + 1 < n_steps)(lambda: gather(step + 1, 1 - slot))
    for k in range(2):
        pltpu.make_async_copy(_token_rows(ys_ref, 0, tb, rpt), buf.at[slot, k], sems.at[slot, k]).wait()
    f = (wcol_ref[:, 0:1] * _load_token_tiles(buf.at[slot, 0], tb)
         + wcol_ref[:, 1:2] * _load_token_tiles(buf.at[slot, 1], tb))
    o_ref[...] = _layer_norm(alpha * x_ref[...] + mod_ref[5:6, :] * f, lng_ref[1:2, :], lnb_ref[1:2, :])


def _combine(dest, ys, wcol, x, mod, lng, lnb, alpha, has_ctx):
    bsz, t_len, d = x.shape
    n_t = t_len // TOKEN_BLOCK
    tok = lambda b, t, off: (b, t, 0)
    blk = lambda b, t, off: (b * n_t + t, 0, 0)
    const2 = lambda b, t, off: (0, 0)
    mod_row = (lambda b, t, off: (jnp.where(t == 0, bsz, b), 0, 0)) if has_ctx else (lambda b, t, off: (b, 0, 0))
    return pl.pallas_call(
        functools.partial(_combine_kernel, alpha=alpha),
        grid_spec=pltpu.PrefetchScalarGridSpec(
            num_scalar_prefetch=1,
            grid=(bsz, n_t),
            in_specs=[
                pl.BlockSpec(memory_space=pl.ANY),
                pl.BlockSpec((None, TOKEN_BLOCK, LANES), blk),
                pl.BlockSpec((None, TOKEN_BLOCK, d), tok),
                pl.BlockSpec((None, 6, d), mod_row),
                pl.BlockSpec(lng.shape, const2),
                pl.BlockSpec(lnb.shape, const2),
            ],
            out_specs=pl.BlockSpec((None, TOKEN_BLOCK, d), tok),
            scratch_shapes=[pltpu.VMEM((2, 2, TOKEN_BLOCK * d // LANES, LANES), F32),
                            pltpu.SemaphoreType.DMA((2, 2))],
        ),
        out_shape=jax.ShapeDtypeStruct(x.shape, F32),
        compiler_params=_params(2),
        name="moe_combine_ln",
    )(dest, ys, wcol, x, mod, lng, lnb)


def _qkv_columns(kind):
    if kind == 0:
        perm, *_ = _slot_layout(A_HEAD_DIM // 4)
        nq, nk = A_HEADS, A_KV_HEADS
        q = np.concatenate([s * LANES + perm for s in range(nq)])
        k = np.concatenate([nq * LANES + s * LANES + perm for s in range(nk)])
        v = np.arange((nq + nk) * LANES, (nq + 2 * nk) * LANES)
        return np.concatenate([q, k, v]), nq, nk, nk
    perm, axis_of, freq_of, half_of, _ = _slot_layout(B_HEAD_DIM // 4)
    nf = B_HEAD_DIM // 4
    if kind == 1:
        nq = B_HEADS // 2
        q = np.concatenate([s * LANES + perm for s in range(nq)])
        k_base = B_HEADS * B_HEAD_DIM
        v_base = k_base + B_KV_HEADS * B_HEAD_DIM
        in_head = axis_of * 2 * nf + half_of * nf + freq_of
        k = np.concatenate([k_base + h * B_HEAD_DIM + in_head for h in range(B_KV_HEADS)])
        v = np.concatenate([v_base + h * B_HEAD_DIM + np.arange(LANES) % B_HEAD_DIM for h in range(B_KV_HEADS)])
        return np.concatenate([q, k, v]), nq, B_KV_HEADS, B_KV_HEADS
    nq = C_HEADS
    q = np.concatenate([s * LANES + perm for s in range(nq)])
    k = np.concatenate([nq * LANES + s * LANES + perm for s in range(nq)])
    v = np.arange(2 * nq * LANES, 3 * nq * LANES)
    return np.concatenate([q, k, v]), nq, nq, nq


def kernel(x, c, ctx, c_ctx, w_mod, b_mod, ln_gain, ln_bias, a_w_qkv, a_q_gain, a_k_gain, a_w_o, b_w_qkv, b_sink, b_w_o, c_w_qkv, c_lam_q1, c_lam_k1, c_lam_q2, c_lam_k2, c_subln_gain, c_w_o, moe_w_group, moe_b_group, moe_w_expert, moe_b_expert, moe_w_gate, moe_w_up, moe_w_down):
    bsz, seq, d = x.shape
    n_ctx = ctx.shape[1]
    depth = w_mod.shape[0]
    assert n_ctx == TOKEN_BLOCK and seq % TOKEN_BLOCK == 0 and seq >= TOKEN_BLOCK + 2 * WINDOW
    assert d % (SUBLANES * LANES) == 0
    t_len = n_ctx + seq
    n_tok = bsz * t_len
    alpha = (2.0 * depth) ** 0.25

    mod_rows = -(-(bsz + 1) // SUBLANES) * SUBLANES
    cc = jnp.concatenate([c, c_ctx[None, :], jnp.zeros((mod_rows - bsz - 1, d), F32)], axis=0)
    mod_all = _modulation(cc, w_mod, b_mod).reshape(depth, mod_rows, 6, d)

    stream = jnp.concatenate([ctx, x], axis=1)
    tables = {hd: _rope_tables(seq, n_ctx, hd // 4, hd ** -0.5 * LOG2_E) for hd in (A_HEAD_DIM, B_HEAD_DIM)}
    perm_a = _slot_layout(A_HEAD_DIM // 4)[0]

    for i in range(depth):
        kind, j = i % N_MIXERS, i // N_MIXERS
        mod = mod_all[i]
        cols, n_q, n_k, n_v = _qkv_columns(kind)
        if kind == 0:
            w = a_w_qkv[j][:, cols].astype(BF16)
            gains = (a_q_gain[j][perm_a][None, :], a_k_gain[j][perm_a][None, :])
            q, k, v = _qkv_call(stream, mod, w, tables[A_HEAD_DIM], gains, n_q, n_k, n_v)
            o = _attn_a(q, k, v, n_ctx)
            w_o = a_w_o[j]
        elif kind == 1:
            w = b_w_qkv[j][:, cols].astype(BF16)
            q, k, v = _qkv_call(stream, mod, w, tables[B_HEAD_DIM], None, n_q, n_k, n_v)
            o = _attn_b(b_sink[j], q, k, v, n_ctx)
            w_o = b_w_o[j]
        else:
            w = c_w_qkv[j][:, cols].astype(BF16)
            q, k, v = _qkv_call(stream, mod, w, tables[C_HEAD_DIM], None, n_q, n_k, n_v)
            lam_init = 0.8 - 0.6 * float(np.exp(-0.3 * i))
            lam_vecs = jnp.stack([c_lam_q1[j], c_lam_k1[j], c_lam_q2[j], c_lam_k2[j]], axis=0)
            o = _attn_c(lam_vecs, c_subln_gain[j][None, :], q, k, v, n_ctx, lam_init)
            w_o = c_w_o[j]

        w_router = jnp.zeros((ROUTER_ROWS, d), F32)
        w_router = w_router.at[0:N_GROUPS].set(moe_w_group[i].T).at[SUBLANES:SUBLANES + N_EXPERTS].set(moe_w_expert[i].T)
        b_router = jnp.zeros((ROUTER_ROWS, 1), F32)
        b_router = b_router.at[0:N_GROUPS, 0].set(moe_b_group[i]).at[SUBLANES:SUBLANES + N_EXPERTS, 0].set(moe_b_expert[i])

        last = i == depth - 1
        stream, h2, rinfo, wcol, counts = _oproj_router(
            o, w_o.astype(BF16), stream, mod, ln_gain[i], ln_bias[i], w_router, b_router, alpha, 1 if last else 0)

        n_tiles = -(-2 * rinfo.shape[0] * TOKEN_BLOCK // EXPERT_TILE) + N_EXPERTS
        xs_rows = n_tiles * EXPERT_TILE
        cnt = counts[:, 0].astype(jnp.int32)
        tiles = (cnt + EXPERT_TILE - 1) // EXPERT_TILE
        tile_end = jnp.cumsum(tiles)
        row_off = (tile_end - tiles) * EXPERT_TILE
        n_used = tile_end[-1:].astype(jnp.int32)
        tile_ids = jnp.arange(n_tiles, dtype=jnp.int32)
        tile_expert = jnp.minimum(
            jnp.sum((tile_end[None, :] <= tile_ids[:, None]).astype(jnp.int32), axis=1), N_EXPERTS - 1)

        ids = jnp.arange(N_EXPERTS, dtype=jnp.int32)
        later_used = jnp.logical_and(ids[None, :] > ids[:, None], tiles[None, :] > 0)
        next_expert = jnp.min(jnp.where(later_used, ids[None, :], N_EXPERTS), axis=1)
        next_expert = jnp.where(next_expert == N_EXPERTS, -1, next_expert).astype(jnp.int32)

        expert_start = jnp.sum(jnp.where(rinfo[:, 0:2, :, None] == ids, row_off, 0), axis=-1)
        dest = (expert_start + rinfo[:, 2:4, :]).reshape(-1)

        xs = _dispatch(row_off, tiles, dest, h2, xs_rows, d // LANES)
        ys = _experts(tile_expert, n_used, next_expert, xs, moe_w_gate, moe_w_up, moe_w_down, i)
        stream = _combine(dest, ys, wcol, stream, mod, ln_gain[i], ln_bias[i], alpha, not last)

    return stream
```

```python
import functools

import numpy as np
import jax
import jax.numpy as jnp
from jax import lax
from jax.experimental import pallas as pl
from jax.experimental.pallas import tpu as pltpu

GRID_W = 64
ROPE_THETA = 10000.0
LN_EPS = 1e-5
RMS_EPS = 1e-6
NEG_INF = -1e30
N_MIXERS = 3
LOG2_E = float(np.log2(np.e))

A_HEADS, A_KV_HEADS, A_HEAD_DIM = 8, 2, 128
B_HEADS, B_KV_HEADS, B_HEAD_DIM = 16, 4, 64
WINDOW = 128
C_HEADS, C_HEAD_DIM = 8, 64
N_GROUPS, EXPERTS_PER_GROUP = 4, 8
N_EXPERTS = N_GROUPS * EXPERTS_PER_GROUP

LANES = 128
SUBLANES = 8
HALF = LANES // 2
TOKEN_BLOCK = 256
EXPERT_TILE = 512
ROUTER_ROWS = 128
ROW_UNROLL = 8
C_HEADS_PER_STEP = 4
VMEM_LIMIT = 56 * 1024 * 1024

F32 = jnp.float32
BF16 = jnp.bfloat16


def _nt_dot(a, b):
    return lax.dot_general(a, b, (((1,), (1,)), ((), ())), preferred_element_type=F32)


def _dot(a, b):
    return jnp.dot(a, b, preferred_element_type=F32)


def _split_bf16(x):
    hi = x.astype(BF16)
    lo = (x - hi.astype(F32)).astype(BF16)
    return hi, lo


def _params(n_axes):
    return pltpu.CompilerParams(dimension_semantics=("arbitrary",) * n_axes, vmem_limit_bytes=VMEM_LIMIT)


def _slot_layout(n_freq):
    n_sub = LANES // (4 * n_freq)
    perm = np.zeros(LANES, np.int32)
    axis_of = np.zeros(LANES, np.int32)
    freq_of = np.zeros(LANES, np.int32)
    half_of = np.zeros(LANES, np.int32)
    sub_of = np.zeros(LANES, np.int32)
    for half in range(2):
        for sub in range(n_sub):
            for axis in range(2):
                for f in range(n_freq):
                    l = half * HALF + sub * 2 * n_freq + axis * n_freq + f
                    perm[l] = sub * 4 * n_freq + axis * 2 * n_freq + half * n_freq + f
                    axis_of[l], freq_of[l], half_of[l], sub_of[l] = axis, f, half, sub
    return perm, axis_of, freq_of, half_of, sub_of


def _rope_tables(seq, ctx, n_freq, q_scale):
    _, axis_of, freq_of, half_of, _ = _slot_layout(n_freq)
    rows = seq // GRID_W
    row = jnp.broadcast_to(jnp.arange(rows, dtype=F32)[:, None], (rows, GRID_W)).reshape(-1)
    col = jnp.broadcast_to(jnp.arange(GRID_W, dtype=F32)[None, :], (rows, GRID_W)).reshape(-1)
    inv = ROPE_THETA ** (-jnp.arange(n_freq, dtype=F32) / n_freq)
    ang = jnp.stack([row[:, None] * inv, col[:, None] * inv], axis=1)
    cos, sin = jnp.cos(ang), jnp.sin(ang)
    cos_l = cos[:, axis_of, freq_of]
    sin_l = sin[:, axis_of, freq_of] * jnp.asarray(np.where(half_of == 0, -1.0, 1.0), F32)
    cos_l = jnp.concatenate([jnp.ones((ctx, LANES), F32), cos_l], axis=0)
    sin_l = jnp.concatenate([jnp.zeros((ctx, LANES), F32), sin_l], axis=0)
    return cos_l * q_scale, sin_l * q_scale, cos_l, sin_l


def _mod_kernel(c_ref, w_ref, b_ref, o_ref):
    c = c_ref[...]
    s = c * (1.0 / (1.0 + jnp.exp(-c)))
    s_hi, s_lo = _split_bf16(s)
    w_hi, w_lo = _split_bf16(w_ref[...])
    o_ref[...] = _dot(s_hi, w_hi) + _dot(s_lo, w_hi) + _dot(s_hi, w_lo) + b_ref[...]


def _modulation(cc, w_mod, b_mod):
    depth, d, n = w_mod.shape
    rows = cc.shape[0]
    tn = n // 4
    return pl.pallas_call(
        _mod_kernel,
        grid=(depth, n // tn),
        in_specs=[
            pl.BlockSpec((rows, d), lambda l, j: (0, 0)),
            pl.BlockSpec((None, d, tn), lambda l, j: (l, 0, j)),
            pl.BlockSpec((None, 1, tn), lambda l, j: (l, 0, j)),
        ],
        out_specs=pl.BlockSpec((None, rows, tn), lambda l, j: (l, 0, j)),
        out_shape=jax.ShapeDtypeStruct((depth, rows, n), F32),
        compiler_params=_params(2),
        name="modulation",
    )(cc, w_mod, b_mod.reshape(depth, 1, n))


def _qkv_kernel(*refs, n_q, n_k, rms):
    if rms:
        x_ref, mod_ref, w_ref, cq_ref, sq_ref, ck_ref, sk_ref, gq_ref, gk_ref, q_ref, k_ref, v_ref = refs
    else:
        x_ref, mod_ref, w_ref, cq_ref, sq_ref, ck_ref, sk_ref, q_ref, k_ref, v_ref = refs
        gq_ref = gk_ref = None
    sh, sc = mod_ref[0:1, :], mod_ref[1:2, :]
    h = (x_ref[...] * (1.0 + sc) + sh).astype(BF16)

    def rope_slots(y, cos, sin, gain_ref, out_ref, n_slots):
        for j in range(n_slots):
            yj = y[:, j * LANES:(j + 1) * LANES]
            if gain_ref is not None:
                ms = jnp.mean(yj * yj, axis=-1, keepdims=True)
                yj = yj * lax.rsqrt(ms + RMS_EPS) * gain_ref[...]
            out = yj * cos + pltpu.roll(yj, HALF, 1) * sin
            out_ref[:, j * LANES:(j + 1) * LANES] = out.astype(BF16)

    nq, nk = n_q * LANES, n_k * LANES
    rope_slots(_dot(h, w_ref[:, 0:nq]), cq_ref[...], sq_ref[...], gq_ref, q_ref, n_q)
    rope_slots(_dot(h, w_ref[:, nq:nq + nk]), ck_ref[...], sk_ref[...], gk_ref, k_ref, n_k)
    v_ref[...] = _dot(h, w_ref[:, nq + nk:]).astype(BF16)


def _qkv_call(x, mod, w, tables, gains, n_q, n_k, n_v):
    bsz, t_len, d = x.shape
    n_t = t_len // TOKEN_BLOCK
    n_rows = mod.shape[0]
    rms = gains is not None
    tok = lambda b, t: (b, t, 0)
    tab_spec = pl.BlockSpec((TOKEN_BLOCK, LANES), lambda b, t: (t, 0))
    in_specs = [
        pl.BlockSpec((None, TOKEN_BLOCK, d), tok),
        pl.BlockSpec((None, 6, d), lambda b, t: (jnp.where(t == 0, bsz, b), 0, 0)),
        pl.BlockSpec(w.shape, lambda b, t: (0, 0)),
        tab_spec, tab_spec, tab_spec, tab_spec,
    ]
    args = [x, mod, w, *tables]
    if rms:
        in_specs += [pl.BlockSpec((1, LANES), lambda b, t: (0, 0))] * 2
        args += list(gains)
    del n_rows
    return pl.pallas_call(
        functools.partial(_qkv_kernel, n_q=n_q, n_k=n_k, rms=rms),
        grid=(bsz, n_t),
        in_specs=in_specs,
        out_specs=[pl.BlockSpec((None, TOKEN_BLOCK, n * LANES), tok) for n in (n_q, n_k, n_v)],
        out_shape=[jax.ShapeDtypeStruct((bsz, t_len, n * LANES), BF16) for n in (n_q, n_k, n_v)],
        compiler_params=_params(2),
        name="qkv_rope",
    )(*args)


def _softmax_pv(s, v_ext):
    e = jnp.exp2(s - jnp.max(s, axis=-1, keepdims=True))
    oe = _dot(e.astype(BF16), v_ext)
    return oe[:, :LANES], oe[:, LANES:]


def _fill_v_ext(v_ref, v_ext):
    v_ext[:, :LANES] = v_ref[...]
    v_ext[:, LANES:] = jnp.ones((v_ref.shape[0], LANES), BF16)


def _attn_a_kernel(q_ref, k_ref, v_ref, o_ref, v_ext, *, ctx, group):
    t = pl.program_id(2)
    pl.when(t == 0)(lambda: _fill_v_ext(v_ref, v_ext))

    def attend(n_keys):
        k, v = k_ref[0:n_keys, :], v_ext[0:n_keys, :]
        for g in range(group):
            num, den = _softmax_pv(_nt_dot(q_ref[:, g * LANES:(g + 1) * LANES], k), v)
            o_ref[:, g * LANES:(g + 1) * LANES] = (num / den).astype(BF16)

    pl.when(t == 0)(lambda: attend(ctx))
    pl.when(t > 0)(lambda: attend(k_ref.shape[0]))


def _attn_a(q, k, v, ctx):
    bsz, t_len, _ = q.shape
    group = A_HEADS // A_KV_HEADS
    qo_spec = pl.BlockSpec((None, TOKEN_BLOCK, group * LANES), lambda b, h, t: (b, t, h))
    kv_spec = pl.BlockSpec((None, t_len, LANES), lambda b, h, t: (b, 0, h))
    return pl.pallas_call(
        functools.partial(_attn_a_kernel, ctx=ctx, group=group),
        grid=(bsz, A_KV_HEADS, t_len // TOKEN_BLOCK),
        in_specs=[qo_spec, kv_spec, kv_spec],
        out_specs=qo_spec,
        out_shape=jax.ShapeDtypeStruct(q.shape, BF16),
        scratch_shapes=[pltpu.VMEM((t_len, 2 * LANES), BF16)],
        compiler_params=_params(3),
        name="attn_dense_gqa",
    )(q, k, v)


def _attn_b_kernel(sink_ref, q_ref, k_ref, v_ref, o_ref, v_ext, *, ctx, seq, group):
    t = pl.program_id(1)
    lane = lax.broadcasted_iota(jnp.int32, (1, LANES), 1)
    first_head = (lane // (HALF // 2)) % 2 == 0
    lower = lane < HALF
    span = TOKEN_BLOCK + 2 * WINDOW
    slots_per_kv = group // 2

    @pl.when(t == 0)
    def _():
        lower_all = lax.broadcasted_iota(jnp.int32, (1, v_ref.shape[1]), 1) % LANES < HALF
        v_ext[...] = jnp.where(lower_all, v_ref[...], jnp.ones_like(v_ref))

    def run(keys, values, valid):
        for h in range(B_KV_HEADS):
            kk, vv = keys(h), values(h)
            for sl in range(h * slots_per_kv, (h + 1) * slots_per_kv):
                qs = q_ref[:, sl * LANES:(sl + 1) * LANES]
                outs = []
                for j in range(2):
                    qm = jnp.where(first_head if j == 0 else jnp.logical_not(first_head), qs, jnp.zeros_like(qs))
                    s = _nt_dot(qm, kk)
                    if valid is not None:
                        s = jnp.concatenate([s[:, :ctx], jnp.where(valid, s[:, ctx:], NEG_INF)], axis=1)
                    snk = sink_ref[sl * 2 + j] * LOG2_E
                    m = jnp.maximum(jnp.max(s, axis=-1, keepdims=True), snk)
                    oe = _dot(jnp.exp2(s - m).astype(BF16), vv)
                    outs.append((oe, pltpu.roll(oe, HALF, 1), jnp.exp2(snk - m)))
                (oa, ra, za), (ob, rb, zb) = outs
                num = jnp.where(lower, oa, rb)
                den = jnp.where(lower, ra + za, ob + zb)
                o_ref[:, sl * LANES:(sl + 1) * LANES] = (num / den).astype(BF16)

    def head(ref, h, rows):
        return ref[rows, h * LANES:(h + 1) * LANES]

    @pl.when(t == 0)
    def _():
        rows = slice(0, ctx)
        run(lambda h: head(k_ref, h, rows), lambda h: head(v_ext, h, rows), None)

    @pl.when(t > 0)
    def _():
        p0 = (t - 1) * TOKEN_BLOCK
        start = pl.multiple_of(jnp.clip(p0 - WINDOW, 0, seq - span), WINDOW)

        def ctx_and_window(ref, h):
            return jnp.concatenate([head(ref, h, slice(0, ctx)), head(ref, h, pl.ds(ctx + start, span))], axis=0)

        dist = (lax.broadcasted_iota(jnp.int32, (TOKEN_BLOCK, span), 1)
                - lax.broadcasted_iota(jnp.int32, (TOKEN_BLOCK, span), 0) + (start - p0))
        run(lambda h: ctx_and_window(k_ref, h), lambda h: ctx_and_window(v_ext, h), jnp.abs(dist) <= WINDOW)


def _attn_b(sink, q, k, v, ctx):
    bsz, t_len, _ = q.shape
    group = B_HEADS // B_KV_HEADS
    qo_spec = pl.BlockSpec((None, TOKEN_BLOCK, q.shape[-1]), lambda b, t, s: (b, t, 0))
    kv_spec = pl.BlockSpec((None, t_len, k.shape[-1]), lambda b, t, s: (b, 0, 0))
    return pl.pallas_call(
        functools.partial(_attn_b_kernel, ctx=ctx, seq=t_len - ctx, group=group),
        grid_spec=pltpu.PrefetchScalarGridSpec(
            num_scalar_prefetch=1,
            grid=(bsz, t_len // TOKEN_BLOCK),
            in_specs=[qo_spec, kv_spec, kv_spec],
            out_specs=qo_spec,
            scratch_shapes=[pltpu.VMEM((t_len, k.shape[-1]), BF16)],
        ),
        out_shape=jax.ShapeDtypeStruct(q.shape, BF16),
        compiler_params=_params(2),
        name="attn_window_sink",
    )(sink, q, k, v)


def _attn_c_kernel(lam_ref, gain_ref, q_ref, k_ref, v_ref, o_ref, *, ctx, lam_init):
    t = pl.program_id(2)
    lane = lax.broadcasted_iota(jnp.int32, (1, LANES), 1)
    first_map = (lane // (HALF // 2)) % 2 == 0
    lam = (jnp.exp(jnp.sum(lam_ref[0:1, :] * lam_ref[1:2, :], axis=-1, keepdims=True))
           - jnp.exp(jnp.sum(lam_ref[2:3, :] * lam_ref[3:4, :], axis=-1, keepdims=True)) + lam_init)

    def attend(n_keys):
        for h in range(q_ref.shape[1] // LANES):
            head = slice(h * LANES, (h + 1) * LANES)
            k, v = k_ref[0:n_keys, head], v_ref[0:n_keys, head]
            q = q_ref[:, head]
            zero = jnp.zeros_like(q)
            s0 = _nt_dot(jnp.where(first_map, q, zero), k)
            s1 = _nt_dot(jnp.where(first_map, zero, q), k)
            e0 = jnp.exp2(s0 - jnp.max(s0, axis=-1, keepdims=True))
            e1 = jnp.exp2(s1 - jnp.max(s1, axis=-1, keepdims=True))
            r0 = 1.0 / jnp.sum(e0, axis=-1, keepdims=True)
            r1 = lam / jnp.sum(e1, axis=-1, keepdims=True)
            o = _dot((e0 * r0 - e1 * r1).astype(BF16), v)
            y = o * lax.rsqrt(jnp.mean(o * o, axis=-1, keepdims=True) + RMS_EPS) * gain_ref[...]
            o_ref[:, head] = (y * (1.0 - lam_init)).astype(BF16)

    pl.when(t == 0)(lambda: attend(ctx))
    pl.when(t > 0)(lambda: attend(k_ref.shape[0]))


def _attn_c(lam_vecs, gain, q, k, v, ctx, lam_init):
    bsz, t_len, _ = q.shape
    width = C_HEADS_PER_STEP * LANES
    qo_spec = pl.BlockSpec((None, TOKEN_BLOCK, width), lambda b, h, t: (b, t, h))
    kv_spec = pl.BlockSpec((None, t_len, width), lambda b, h, t: (b, 0, h))
    return pl.pallas_call(
        functools.partial(_attn_c_kernel, ctx=ctx, lam_init=lam_init),
        grid=(bsz, C_HEADS // C_HEADS_PER_STEP, t_len // TOKEN_BLOCK),
        in_specs=[pl.BlockSpec(lam_vecs.shape, lambda b, h, t: (0, 0)),
                  pl.BlockSpec(gain.shape, lambda b, h, t: (0, 0)),
                  qo_spec, kv_spec, kv_spec],
        out_specs=qo_spec,
        out_shape=jax.ShapeDtypeStruct(q.shape, BF16),
        compiler_params=_params(3),
        name="attn_differential",
    )(lam_vecs, gain, q, k, v)


def _store_token_tiles(ref, x):
    rows, d = x.shape
    rpt = d // LANES
    for c in range(rpt):
        ref[pl.ds(c, rows, stride=rpt), :] = x[:, c * LANES:(c + 1) * LANES]


def _load_token_tiles(ref, rows):
    rpt = ref.shape[0] // rows
    return jnp.concatenate([ref[pl.ds(c, rows, stride=rpt), :] for c in range(rpt)], axis=1)


def _layer_norm(z, gain, bias):
    mu = jnp.mean(z, axis=-1, keepdims=True)
    zc = z - mu
    var = jnp.mean(zc * zc, axis=-1, keepdims=True)
    return zc * lax.rsqrt(var + LN_EPS) * gain + bias


def _oproj_router_kernel(o_ref, wo_ref, x_ref, mod_ref, lng_ref, lnb_ref, wr_ref, br_ref,
                         xo_ref, h2_ref, ri_ref, wcol_ref, cnt_ref, carry_ref, *, alpha):
    @pl.when(jnp.logical_and(pl.program_id(0) == 0, pl.program_id(1) == 0))
    def _():
        carry_ref[...] = jnp.zeros_like(carry_ref)

    tb = x_ref.shape[0]
    y = _dot(o_ref[...], wo_ref[...])
    xn = _layer_norm(alpha * x_ref[...] + mod_ref[2:3, :] * y, lng_ref[0:1, :], lnb_ref[0:1, :])
    xo_ref[...] = xn
    h2 = xn * (1.0 + mod_ref[4:5, :]) + mod_ref[3:4, :]
    h_hi, h_lo = _split_bf16(h2)
    _store_token_tiles(h2_ref, h_hi.astype(F32))
    w_hi, w_lo = _split_bf16(wr_ref[...])
    logits = _nt_dot(w_hi, h_hi) + _nt_dot(w_hi, h_lo) + _nt_dot(w_lo, h_hi) + br_ref[...]

    row8 = lax.broadcasted_iota(jnp.int32, (SUBLANES, tb), 0)
    glog = jnp.where(row8 < N_GROUPS, logits[0:SUBLANES, :], NEG_INF)
    gmax = jnp.max(glog, axis=0, keepdims=True)
    gsel = jnp.min(jnp.where(glog == gmax, row8, SUBLANES), axis=0, keepdims=True)
    gprob = 1.0 / jnp.sum(jnp.exp(glog - gmax), axis=0, keepdims=True)
    elog = jnp.zeros((SUBLANES, tb), F32)
    for g in range(N_GROUPS):
        elog = jnp.where(gsel == g, logits[(g + 1) * SUBLANES:(g + 2) * SUBLANES, :], elog)
    v1 = jnp.max(elog, axis=0, keepdims=True)
    i1 = jnp.min(jnp.where(elog == v1, row8, SUBLANES), axis=0, keepdims=True)
    elog2 = jnp.where(row8 == i1, NEG_INF, elog)
    v2 = jnp.max(elog2, axis=0, keepdims=True)
    i2 = jnp.min(jnp.where(elog2 == v2, row8, SUBLANES), axis=0, keepdims=True)
    t21 = jnp.exp(v2 - v1)
    w1 = gprob / (1.0 + t21)
    w2 = gprob * t21 / (1.0 + t21)
    e1 = gsel * EXPERTS_PER_GROUP + i1
    e2 = gsel * EXPERTS_PER_GROUP + i2

    rowe = lax.broadcasted_iota(jnp.int32, (N_EXPERTS, tb), 0)
    oh1, oh2 = rowe == e1, rowe == e2
    oh = jnp.where(jnp.logical_or(oh1, oh2), 1.0, 0.0)
    before = (lax.broadcasted_iota(jnp.int32, (tb, tb), 0) < lax.broadcasted_iota(jnp.int32, (tb, tb), 1))
    base = _dot(oh.astype(BF16), jnp.where(before, 1.0, 0.0).astype(BF16)) + carry_ref[:, 0:1]
    rank1 = jnp.sum(jnp.where(oh1, base, 0.0), axis=0, keepdims=True).astype(jnp.int32)
    rank2 = jnp.sum(jnp.where(oh2, base, 0.0), axis=0, keepdims=True).astype(jnp.int32)
    carry_ref[...] = carry_ref[...] + jnp.sum(oh, axis=1, keepdims=True)
    cnt_ref[...] = carry_ref[...]

    ri_ref[...] = jnp.where(row8 == 0, e1, jnp.where(row8 == 1, e2, jnp.where(row8 == 2, rank1,
                            jnp.where(row8 == 3, rank2, 0))))
    rowl = lax.broadcasted_iota(jnp.int32, (LANES, tb), 0)
    wcol_ref[...] = jnp.where(rowl == 0, w1, jnp.where(rowl == 1, w2, 0.0)).T


def _oproj_router(o, w_o, x, mod, lng, lnb, w_router, b_router, alpha, first_block):
    bsz, t_in, d = x.shape
    n_t = t_in // TOKEN_BLOCK - first_block
    t_len = n_t * TOKEN_BLOCK
    n_blk = bsz * n_t
    tok_in = lambda b, t: (b, t + first_block, 0)
    tok = lambda b, t: (b, t, 0)
    blk = lambda b, t: (b * n_t + t, 0, 0)
    const2 = lambda b, t: (0, 0)
    return pl.pallas_call(
        functools.partial(_oproj_router_kernel, alpha=alpha),
        grid=(bsz, n_t),
        in_specs=[
            pl.BlockSpec((None, TOKEN_BLOCK, o.shape[-1]), tok_in),
            pl.BlockSpec(w_o.shape, const2),
            pl.BlockSpec((None, TOKEN_BLOCK, d), tok_in),
            pl.BlockSpec((None, 6, d), lambda b, t: (jnp.where(t + first_block == 0, bsz, b), 0, 0)),
            pl.BlockSpec(lng.shape, const2),
            pl.BlockSpec(lnb.shape, const2),
            pl.BlockSpec(w_router.shape, const2),
            pl.BlockSpec(b_router.shape, const2),
        ],
        out_specs=[
            pl.BlockSpec((None, TOKEN_BLOCK, d), tok),
            pl.BlockSpec((TOKEN_BLOCK * d // LANES, LANES), lambda b, t: (b * n_t + t, 0)),
            pl.BlockSpec((None, SUBLANES, TOKEN_BLOCK), blk),
            pl.BlockSpec((None, TOKEN_BLOCK, LANES), blk),
            pl.BlockSpec((N_EXPERTS, LANES), const2),
        ],
        out_shape=[
            jax.ShapeDtypeStruct((bsz, t_len, d), F32),
            jax.ShapeDtypeStruct((bsz * t_len * d // LANES, LANES), F32),
            jax.ShapeDtypeStruct((n_blk, SUBLANES, TOKEN_BLOCK), jnp.int32),
            jax.ShapeDtypeStruct((n_blk, TOKEN_BLOCK, LANES), F32),
            jax.ShapeDtypeStruct((N_EXPERTS, LANES), F32),
        ],
        scratch_shapes=[pltpu.VMEM((N_EXPERTS, LANES), F32)],
        compiler_params=_params(2),
        name="oproj_ln_router",
    )(o, w_o, x, mod, lng, lnb, w_router, b_router)


def _for_each_row(n_rows, fn):
    def chunk(c, carry):
        for u in range(ROW_UNROLL):
            fn(c * ROW_UNROLL + u)
        return carry

    lax.fori_loop(0, n_rows // ROW_UNROLL, chunk, 0)


def _expert_row(dest_ref, block, k, r):
    return dest_ref[(block * 2 + k) * TOKEN_BLOCK + r]


def _token_rows(ref, row, n, rpt):
    return ref.at[pl.ds(pl.multiple_of(row * rpt, rpt), n * rpt)]


def _dispatch_kernel(off_ref, tiles_ref, dest_ref, h_ref, xs_ref, zeros, sems, *, rpt):
    tb = h_ref.shape[0] // rpt
    block = pl.program_id(0)

    @pl.when(pl.program_id(0) == 0)
    def _():
        zeros[...] = jnp.zeros_like(zeros)

        def last_tile(e):
            return pltpu.make_async_copy(
                zeros, _token_rows(xs_ref, off_ref[e] + (tiles_ref[e] - 1) * EXPERT_TILE, EXPERT_TILE, rpt), sems.at[2])

        for e in range(N_EXPERTS):
            pl.when(tiles_ref[e] > 0)(lambda e=e: last_tile(e).start())
        for e in range(N_EXPERTS):
            pl.when(tiles_ref[e] > 0)(lambda e=e: last_tile(e).wait())

        def spare_tile(i):
            return pltpu.make_async_copy(zeros, _token_rows(xs_ref, i * EXPERT_TILE, EXPERT_TILE, rpt), sems.at[2])

        first_spare = off_ref[N_EXPERTS - 1] // EXPERT_TILE + tiles_ref[N_EXPERTS - 1]
        n_tiles = xs_ref.shape[0] // (EXPERT_TILE * rpt)
        lax.fori_loop(first_spare, n_tiles, lambda i, c: (spare_tile(i).start(), c)[1], 0)
        lax.fori_loop(first_spare, n_tiles, lambda i, c: (spare_tile(i).wait(), c)[1], 0)

    for sub in range(tb // TOKEN_BLOCK):
        def start(r, sub=sub):
            for k in range(2):
                pltpu.make_async_copy(
                    _token_rows(h_ref, sub * TOKEN_BLOCK + r, 1, rpt),
                    _token_rows(xs_ref, _expert_row(dest_ref, block * (tb // TOKEN_BLOCK) + sub, k, r), 1, rpt),
                    sems.at[k]).start(priority=k)

        _for_each_row(TOKEN_BLOCK, start)
    for k in range(2):
        pltpu.make_async_copy(h_ref, _token_rows(xs_ref, 0, tb, rpt), sems.at[k]).wait()


def _dispatch(row_off, tiles, dest, h2, n_rows, rpt):
    n_blk = h2.shape[0] // (TOKEN_BLOCK * rpt)
    per_step = 2 if n_blk % 2 == 0 else 1
    return pl.pallas_call(
        functools.partial(_dispatch_kernel, rpt=rpt),
        grid_spec=pltpu.PrefetchScalarGridSpec(
            num_scalar_prefetch=3,
            grid=(n_blk // per_step,),
            in_specs=[pl.BlockSpec((per_step * TOKEN_BLOCK * rpt, LANES), lambda i, off, nt, dst: (i, 0))],
            out_specs=pl.BlockSpec(memory_space=pl.ANY),
            scratch_shapes=[pltpu.VMEM((EXPERT_TILE * rpt, LANES), F32), pltpu.SemaphoreType.DMA((3,))],
        ),
        out_shape=jax.ShapeDtypeStruct((n_rows * rpt, LANES), F32),
        compiler_params=_params(1),
        name="moe_dispatch",
    )(row_off, tiles, dest, h2)


def _experts_kernel(te_ref, nu_ref, nxt_ref, xs_ref, wg_hbm, wu_hbm, wd_hbm, ys_ref,
                    wg_f, wu_f, wd_f, wg_s, wu_s, wd_s, slot_ref, sems, *, layer):
    t = pl.program_id(0)

    def fetch(e, slot):
        g, i = e // EXPERTS_PER_GROUP, e % EXPERTS_PER_GROUP
        return [pltpu.make_async_copy(w.at[layer, g, i], buf.at[slot], sems.at[slot, n])
                for n, (w, buf) in enumerate(((wg_hbm, wg_f), (wu_hbm, wu_f), (wd_hbm, wd_f)))]

    @pl.when(t < nu_ref[0])
    def _():
        e = te_ref[t]

        @pl.when(t == 0)
        def _():
            slot_ref[0] = 0
            for c in fetch(e, 0):
                c.start()

        @pl.when(jnp.logical_or(t == 0, e != te_ref[jnp.maximum(t - 1, 0)]))
        def _():
            slot = slot_ref[0]
            for c in fetch(e, slot):
                c.wait()
            nxt = nxt_ref[e]

            @pl.when(nxt >= 0)
            def _():
                for c in fetch(nxt, 1 - slot):
                    c.start()

            wg_s[...] = wg_f[slot].astype(BF16)
            wu_s[...] = wu_f[slot].astype(BF16)
            wd_s[...] = wd_f[slot].astype(BF16)
            slot_ref[0] = 1 - slot

        x = _load_token_tiles(xs_ref, EXPERT_TILE).astype(BF16)
        a = _dot(x, wg_s[...])
        u = _dot(x, wu_s[...])
        hid = a * (1.0 / (1.0 + jnp.exp(-a))) * u
        _store_token_tiles(ys_ref, _dot(hid.astype(BF16), wd_s[...]))

    @pl.when(t >= nu_ref[0])
    def _():
        ys_ref[...] = jnp.zeros_like(ys_ref)


def _experts(tile_expert, n_used, next_expert, xs, w_gate, w_up, w_down, layer):
    d, ff = w_gate.shape[-2:]
    tile_rows = EXPERT_TILE * d // LANES
    n_tiles = xs.shape[0] // tile_rows
    hbm = pl.BlockSpec(memory_space=pl.ANY)
    return pl.pallas_call(
        functools.partial(_experts_kernel, layer=layer),
        grid_spec=pltpu.PrefetchScalarGridSpec(
            num_scalar_prefetch=3,
            grid=(n_tiles,),
            in_specs=[
                pl.BlockSpec((tile_rows, LANES), lambda t, te, nu, nx: (jnp.minimum(t, nu[0] - 1), 0)),
                hbm, hbm, hbm,
            ],
            out_specs=pl.BlockSpec((tile_rows, LANES), lambda t, te, nu, nx: (t, 0)),
            scratch_shapes=[
                pltpu.VMEM((2, d, ff), F32), pltpu.VMEM((2, d, ff), F32), pltpu.VMEM((2, ff, d), F32),
                pltpu.VMEM((d, ff), BF16), pltpu.VMEM((d, ff), BF16), pltpu.VMEM((ff, d), BF16),
                pltpu.SMEM((1,), jnp.int32), pltpu.SemaphoreType.DMA((2, 3)),
            ],
        ),
        out_shape=jax.ShapeDtypeStruct(xs.shape, F32),
        compiler_params=_params(1),
        name="moe_experts",
    )(tile_expert, n_used, next_expert, xs, w_gate, w_up, w_down)


def _combine_kernel(dest_ref, ys_ref, wcol_ref, x_ref, mod_ref, lng_ref, lnb_ref, o_ref, buf, sems, *, alpha):
    tb, d = x_ref.shape
    rpt = d // LANES
    step = pl.program_id(0) * pl.num_programs(1) + pl.program_id(1)
    n_steps = pl.num_programs(0) * pl.num_programs(1)
    slot = step % 2

    def gather(block, into):
        def start(r):
            for k in range(2):
                pltpu.make_async_copy(_token_rows(ys_ref, _expert_row(dest_ref, block, k, r), 1, rpt),
                                      _token_rows(buf.at[into, k], r, 1, rpt), sems.at[into, k]).start(priority=k)

        _for_each_row(tb, start)

    pl.when(step == 0)(lambda: gather(step, 0))
    pl.when(step + 1 < n_steps)(lambda: gather(step + 1, 1 - slot))
    for k in range(2):
        pltpu.make_async_copy(_token_rows(ys_ref, 0, tb, rpt), buf.at[slot, k], sems.at[slot, k]).wait()
    f = (wcol_ref[:, 0:1] * _load_token_tiles(buf.at[slot, 0], tb)
         + wcol_ref[:, 1:2] * _load_token_tiles(buf.at[slot, 1], tb))
    o_ref[...] = _layer_norm(alpha * x_ref[...] + mod_ref[5:6, :] * f, lng_ref[1:2, :], lnb_ref[1:2, :])


def _combine(dest, ys, wcol, x, mod, lng, lnb, alpha, has_ctx):
    bsz, t_len, d = x.shape
    n_t = t_len // TOKEN_BLOCK
    tok = lambda b, t, off: (b, t, 0)
    blk = lambda b, t, off: (b * n_t + t, 0, 0)
    const2 = lambda b, t, off: (0, 0)
    mod_row = (lambda b, t, off: (jnp.where(t == 0, bsz, b), 0, 0)) if has_ctx else (lambda b, t, off: (b, 0, 0))
    return pl.pallas_call(
        functools.partial(_combine_kernel, alpha=alpha),
        grid_spec=pltpu.PrefetchScalarGridSpec(
            num_scalar_prefetch=1,
            grid=(bsz, n_t),
            in_specs=[
                pl.BlockSpec(memory_space=pl.ANY),
                pl.BlockSpec((None, TOKEN_BLOCK, LANES), blk),
                pl.BlockSpec((None, TOKEN_BLOCK, d), tok),
                pl.BlockSpec((None, 6, d), mod_row),
                pl.BlockSpec(lng.shape, const2),
                pl.BlockSpec(lnb.shape, const2),
            ],
            out_specs=pl.BlockSpec((None, TOKEN_BLOCK, d), tok),
            scratch_shapes=[pltpu.VMEM((2, 2, TOKEN_BLOCK * d // LANES, LANES), F32),
                            pltpu.SemaphoreType.DMA((2, 2))],
        ),
        out_shape=jax.ShapeDtypeStruct(x.shape, F32),
        compiler_params=_params(2),
        name="moe_combine_ln",
    )(dest, ys, wcol, x, mod, lng, lnb)


def _qkv_columns(kind):
    if kind == 0:
        perm, *_ = _slot_layout(A_HEAD_DIM // 4)
        nq, nk = A_HEADS, A_KV_HEADS
        q = np.concatenate([s * LANES + perm for s in range(nq)])
        k = np.concatenate([nq * LANES + s * LANES + perm for s in range(nk)])
        v = np.arange((nq + nk) * LANES, (nq + 2 * nk) * LANES)
        return np.concatenate([q, k, v]), nq, nk, nk
    perm, axis_of, freq_of, half_of, _ = _slot_layout(B_HEAD_DIM // 4)
    nf = B_HEAD_DIM // 4
    if kind == 1:
        nq = B_HEADS // 2
        q = np.concatenate([s * LANES + perm for s in range(nq)])
        k_base = B_HEADS * B_HEAD_DIM
        v_base = k_base + B_KV_HEADS * B_HEAD_DIM
        in_head = axis_of * 2 * nf + half_of * nf + freq_of
        k = np.concatenate([k_base + h * B_HEAD_DIM + in_head for h in range(B_KV_HEADS)])
        v = np.concatenate([v_base + h * B_HEAD_DIM + np.arange(LANES) % B_HEAD_DIM for h in range(B_KV_HEADS)])
        return np.concatenate([q, k, v]), nq, B_KV_HEADS, B_KV_HEADS
    nq = C_HEADS
    q = np.concatenate([s * LANES + perm for s in range(nq)])
    k = np.concatenate([nq * LANES + s * LANES + perm for s in range(nq)])
    v = np.arange(2 * nq * LANES, 3 * nq * LANES)
    return np.concatenate([q, k, v]), nq, nq, nq


def kernel(x, c, ctx, c_ctx, w_mod, b_mod, ln_gain, ln_bias, a_w_qkv, a_q_gain, a_k_gain, a_w_o, b_w_qkv, b_sink, b_w_o, c_w_qkv, c_lam_q1, c_lam_k1, c_lam_q2, c_lam_k2, c_subln_gain, c_w_o, moe_w_group, moe_b_group, moe_w_expert, moe_b_expert, moe_w_gate, moe_w_up, moe_w_down):
    bsz, seq, d = x.shape
    n_ctx = ctx.shape[1]
    depth = w_mod.shape[0]
    assert n_ctx == TOKEN_BLOCK and seq % TOKEN_BLOCK == 0 and seq >= TOKEN_BLOCK + 2 * WINDOW
    assert d % (SUBLANES * LANES) == 0
    t_len = n_ctx + seq
    n_tok = bsz * t_len
    alpha = (2.0 * depth) ** 0.25

    mod_rows = -(-(bsz + 1) // SUBLANES) * SUBLANES
    cc = jnp.concatenate([c, c_ctx[None, :], jnp.zeros((mod_rows - bsz - 1, d), F32)], axis=0)
    mod_all = _modulation(cc, w_mod, b_mod).reshape(depth, mod_rows, 6, d)

    stream = jnp.concatenate([ctx, x], axis=1)
    tables = {hd: _rope_tables(seq, n_ctx, hd // 4, hd ** -0.5 * LOG2_E) for hd in (A_HEAD_DIM, B_HEAD_DIM)}
    perm_a = _slot_layout(A_HEAD_DIM // 4)[0]

    for i in range(depth):
        kind, j = i % N_MIXERS, i // N_MIXERS
        mod = mod_all[i]
        cols, n_q, n_k, n_v = _qkv_columns(kind)
        if kind == 0:
            w = a_w_qkv[j][:, cols].astype(BF16)
            gains = (a_q_gain[j][perm_a][None, :], a_k_gain[j][perm_a][None, :])
            q, k, v = _qkv_call(stream, mod, w, tables[A_HEAD_DIM], gains, n_q, n_k, n_v)
            o = _attn_a(q, k, v, n_ctx)
            w_o = a_w_o[j]
        elif kind == 1:
            w = b_w_qkv[j][:, cols].astype(BF16)
            q, k, v = _qkv_call(stream, mod, w, tables[B_HEAD_DIM], None, n_q, n_k, n_v)
            o = _attn_b(b_sink[j], q, k, v, n_ctx)
            w_o = b_w_o[j]
        else:
            w = c_w_qkv[j][:, cols].astype(BF16)
            q, k, v = _qkv_call(stream, mod, w, tables[C_HEAD_DIM], None, n_q, n_k, n_v)
            lam_init = 0.8 - 0.6 * float(np.exp(-0.3 * i))
            lam_vecs = jnp.stack([c_lam_q1[j], c_lam_k1[j], c_lam_q2[j], c_lam_k2[j]], axis=0)
            o = _attn_c(lam_vecs, c_subln_gain[j][None, :], q, k, v, n_ctx, lam_init)
            w_o = c_w_o[j]

        w_router = jnp.zeros((ROUTER_ROWS, d), F32)
        w_router = w_router.at[0:N_GROUPS].set(moe_w_group[i].T).at[SUBLANES:SUBLANES + N_EXPERTS].set(moe_w_expert[i].T)
        b_router = jnp.zeros((ROUTER_ROWS, 1), F32)
        b_router = b_router.at[0:N_GROUPS, 0].set(moe_b_group[i]).at[SUBLANES:SUBLANES + N_EXPERTS, 0].set(moe_b_expert[i])

        last = i == depth - 1
        stream, h2, rinfo, wcol, counts = _oproj_router(
            o, w_o.astype(BF16), stream, mod, ln_gain[i], ln_bias[i], w_router, b_router, alpha, 1 if last else 0)

        n_tiles = -(-2 * rinfo.shape[0] * TOKEN_BLOCK // EXPERT_TILE) + N_EXPERTS
        xs_rows = n_tiles * EXPERT_TILE
        cnt = counts[:, 0].astype(jnp.int32)
        tiles = (cnt + EXPERT_TILE - 1) // EXPERT_TILE
        tile_end = jnp.cumsum(tiles)
        row_off = (tile_end - tiles) * EXPERT_TILE
        n_used = tile_end[-1:].astype(jnp.int32)
        tile_ids = jnp.arange(n_tiles, dtype=jnp.int32)
        tile_expert = jnp.minimum(
            jnp.sum((tile_end[None, :] <= tile_ids[:, None]).astype(jnp.int32), axis=1), N_EXPERTS - 1)

        ids = jnp.arange(N_EXPERTS, dtype=jnp.int32)
        later_used = jnp.logical_and(ids[None, :] > ids[:, None], tiles[None, :] > 0)
        next_expert = jnp.min(jnp.where(later_used, ids[None, :], N_EXPERTS), axis=1)
        next_expert = jnp.where(next_expert == N_EXPERTS, -1, next_expert).astype(jnp.int32)

        expert_start = jnp.sum(jnp.where(rinfo[:, 0:2, :, None] == ids, row_off, 0), axis=-1)
        dest = (expert_start + rinfo[:, 2:4, :]).reshape(-1)

        xs = _dispatch(row_off, tiles, dest, h2, xs_rows, d // LANES)
        ys = _experts(tile_expert, n_used, next_expert, xs, moe_w_gate, moe_w_up, moe_w_down, i)
        stream = _combine(dest, ys, wcol, stream, mod, ln_gain[i], ln_bias[i], alpha, not last)

    return stream
```

```python
import functools

import numpy as np
import jax
import jax.numpy as jnp
from jax import lax
from jax.experimental import pallas as pl
from jax.experimental.pallas import tpu as pltpu

GRID_W = 64
ROPE_THETA = 10000.0
LN_EPS = 1e-5
RMS_EPS = 1e-6
NEG_INF = -1e30
N_MIXERS = 3
LOG2_E = float(np.log2(np.e))

A_HEADS, A_KV_HEADS, A_HEAD_DIM = 8, 2, 128
B_HEADS, B_KV_HEADS, B_HEAD_DIM = 16, 4, 64
WINDOW = 128
C_HEADS, C_HEAD_DIM = 8, 64
N_GROUPS, EXPERTS_PER_GROUP = 4, 8
N_EXPERTS = N_GROUPS * EXPERTS_PER_GROUP

LANES = 128
SUBLANES = 8
HALF = LANES // 2
TOKEN_BLOCK = 256
EXPERT_TILE = 512
ROUTER_ROWS = 128
ROW_UNROLL = 8
C_HEADS_PER_STEP = 4
VMEM_LIMIT = 56 * 1024 * 1024

F32 = jnp.float32
BF16 = jnp.bfloat16


def _nt_dot(a, b):
    return lax.dot_general(a, b, (((1,), (1,)), ((), ())), preferred_element_type=F32)


def _dot(a, b):
    return jnp.dot(a, b, preferred_element_type=F32)


def _split_bf16(x):
    hi = x.astype(BF16)
    lo = (x - hi.astype(F32)).astype(BF16)
    return hi, lo


def _params(n_axes):
    return pltpu.CompilerParams(dimension_semantics=("arbitrary",) * n_axes, vmem_limit_bytes=VMEM_LIMIT)


def _slot_layout(n_freq):
    n_sub = LANES // (4 * n_freq)
    perm = np.zeros(LANES, np.int32)
    axis_of = np.zeros(LANES, np.int32)
    freq_of = np.zeros(LANES, np.int32)
    half_of = np.zeros(LANES, np.int32)
    sub_of = np.zeros(LANES, np.int32)
    for half in range(2):
        for sub in range(n_sub):
            for axis in range(2):
                for f in range(n_freq):
                    l = half * HALF + sub * 2 * n_freq + axis * n_freq + f
                    perm[l] = sub * 4 * n_freq + axis * 2 * n_freq + half * n_freq + f
                    axis_of[l], freq_of[l], half_of[l], sub_of[l] = axis, f, half, sub
    return perm, axis_of, freq_of, half_of, sub_of


def _rope_tables(seq, ctx, n_freq, q_scale):
    _, axis_of, freq_of, half_of, _ = _slot_layout(n_freq)
    rows = seq // GRID_W
    row = jnp.broadcast_to(jnp.arange(rows, dtype=F32)[:, None], (rows, GRID_W)).reshape(-1)
    col = jnp.broadcast_to(jnp.arange(GRID_W, dtype=F32)[None, :], (rows, GRID_W)).reshape(-1)
    inv = ROPE_THETA ** (-jnp.arange(n_freq, dtype=F32) / n_freq)
    ang = jnp.stack([row[:, None] * inv, col[:, None] * inv], axis=1)
    cos, sin = jnp.cos(ang), jnp.sin(ang)
    cos_l = cos[:, axis_of, freq_of]
    sin_l = sin[:, axis_of, freq_of] * jnp.asarray(np.where(half_of == 0, -1.0, 1.0), F32)
    cos_l = jnp.concatenate([jnp.ones((ctx, LANES), F32), cos_l], axis=0)
    sin_l = jnp.concatenate([jnp.zeros((ctx, LANES), F32), sin_l], axis=0)
    return cos_l * q_scale, sin_l * q_scale, cos_l, sin_l


def _mod_kernel(c_ref, w_ref, b_ref, o_ref):
    c = c_ref[...]
    s = c * (1.0 / (1.0 + jnp.exp(-c)))
    s_hi, s_lo = _split_bf16(s)
    w_hi, w_lo = _split_bf16(w_ref[...])
    o_ref[...] = _dot(s_hi, w_hi) + _dot(s_lo, w_hi) + _dot(s_hi, w_lo) + b_ref[...]


def _modulation(cc, w_mod, b_mod):
    depth, d, n = w_mod.shape
    rows = cc.shape[0]
    tn = n // 4
    return pl.pallas_call(
        _mod_kernel,
        grid=(depth, n // tn),
        in_specs=[
            pl.BlockSpec((rows, d), lambda l, j: (0, 0)),
            pl.BlockSpec((None, d, tn), lambda l, j: (l, 0, j)),
            pl.BlockSpec((None, 1, tn), lambda l, j: (l, 0, j)),
        ],
        out_specs=pl.BlockSpec((None, rows, tn), lambda l, j: (l, 0, j)),
        out_shape=jax.ShapeDtypeStruct((depth, rows, n), F32),
        compiler_params=_params(2),
        name="modulation",
    )(cc, w_mod, b_mod.reshape(depth, 1, n))


def _qkv_kernel(*refs, n_q, n_k, rms):
    if rms:
        x_ref, mod_ref, w_ref, cq_ref, sq_ref, ck_ref, sk_ref, gq_ref, gk_ref, q_ref, k_ref, v_ref = refs
    else:
        x_ref, mod_ref, w_ref, cq_ref, sq_ref, ck_ref, sk_ref, q_ref, k_ref, v_ref = refs
        gq_ref = gk_ref = None
    sh, sc = mod_ref[0:1, :], mod_ref[1:2, :]
    h = (x_ref[...] * (1.0 + sc) + sh).astype(BF16)

    def rope_slots(y, cos, sin, gain_ref, out_ref, n_slots):
        for j in range(n_slots):
            yj = y[:, j * LANES:(j + 1) * LANES]
            if gain_ref is not None:
                ms = jnp.mean(yj * yj, axis=-1, keepdims=True)
                yj = yj * lax.rsqrt(ms + RMS_EPS) * gain_ref[...]
            out = yj * cos + pltpu.roll(yj, HALF, 1) * sin
            out_ref[:, j * LANES:(j + 1) * LANES] = out.astype(BF16)

    nq, nk = n_q * LANES, n_k * LANES
    rope_slots(_dot(h, w_ref[:, 0:nq]), cq_ref[...], sq_ref[...], gq_ref, q_ref, n_q)
    rope_slots(_dot(h, w_ref[:, nq:nq + nk]), ck_ref[...], sk_ref[...], gk_ref, k_ref, n_k)
    v_ref[...] = _dot(h, w_ref[:, nq + nk:]).astype(BF16)


def _qkv_call(x, mod, w, tables, gains, n_q, n_k, n_v):
    bsz, t_len, d = x.shape
    n_t = t_len // TOKEN_BLOCK
    n_rows = mod.shape[0]
    rms = gains is not None
    tok = lambda b, t: (b, t, 0)
    tab_spec = pl.BlockSpec((TOKEN_BLOCK, LANES), lambda b, t: (t, 0))
    in_specs = [
        pl.BlockSpec((None, TOKEN_BLOCK, d), tok),
        pl.BlockSpec((None, 6, d), lambda b, t: (jnp.where(t == 0, bsz, b), 0, 0)),
        pl.BlockSpec(w.shape, lambda b, t: (0, 0)),
        tab_spec, tab_spec, tab_spec, tab_spec,
    ]
    args = [x, mod, w, *tables]
    if rms:
        in_specs += [pl.BlockSpec((1, LANES), lambda b, t: (0, 0))] * 2
        args += list(gains)
    del n_rows
    return pl.pallas_call(
        functools.partial(_qkv_kernel, n_q=n_q, n_k=n_k, rms=rms),
        grid=(bsz, n_t),
        in_specs=in_specs,
        out_specs=[pl.BlockSpec((None, TOKEN_BLOCK, n * LANES), tok) for n in (n_q, n_k, n_v)],
        out_shape=[jax.ShapeDtypeStruct((bsz, t_len, n * LANES), BF16) for n in (n_q, n_k, n_v)],
        compiler_params=_params(2),
        name="qkv_rope",
    )(*args)


def _softmax_pv(s, v_ext):
    e = jnp.exp2(s - jnp.max(s, axis=-1, keepdims=True))
    oe = _dot(e.astype(BF16), v_ext)
    return oe[:, :LANES], oe[:, LANES:]


def _fill_v_ext(v_ref, v_ext):
    v_ext[:, :LANES] = v_ref[...]
    v_ext[:, LANES:] = jnp.ones((v_ref.shape[0], LANES), BF16)


def _attn_a_kernel(q_ref, k_ref, v_ref, o_ref, v_ext, *, ctx, group):
    t = pl.program_id(2)
    pl.when(t == 0)(lambda: _fill_v_ext(v_ref, v_ext))

    def attend(n_keys):
        k, v = k_ref[0:n_keys, :], v_ext[0:n_keys, :]
        for g in range(group):
            num, den = _softmax_pv(_nt_dot(q_ref[:, g * LANES:(g + 1) * LANES], k), v)
            o_ref[:, g * LANES:(g + 1) * LANES] = (num / den).astype(BF16)

    pl.when(t == 0)(lambda: attend(ctx))
    pl.when(t > 0)(lambda: attend(k_ref.shape[0]))


def _attn_a(q, k, v, ctx):
    bsz, t_len, _ = q.shape
    group = A_HEADS // A_KV_HEADS
    qo_spec = pl.BlockSpec((None, TOKEN_BLOCK, group * LANES), lambda b, h, t: (b, t, h))
    kv_spec = pl.BlockSpec((None, t_len, LANES), lambda b, h, t: (b, 0, h))
    return pl.pallas_call(
        functools.partial(_attn_a_kernel, ctx=ctx, group=group),
        grid=(bsz, A_KV_HEADS, t_len // TOKEN_BLOCK),
        in_specs=[qo_spec, kv_spec, kv_spec],
        out_specs=qo_spec,
        out_shape=jax.ShapeDtypeStruct(q.shape, BF16),
        scratch_shapes=[pltpu.VMEM((t_len, 2 * LANES), BF16)],
        compiler_params=_params(3),
        name="attn_dense_gqa",
    )(q, k, v)


def _attn_b_kernel(sink_ref, q_ref, k_ref, v_ref, o_ref, v_ext, *, ctx, seq, group):
    t = pl.program_id(1)
    lane = lax.broadcasted_iota(jnp.int32, (1, LANES), 1)
    first_head = (lane // (HALF // 2)) % 2 == 0
    lower = lane < HALF
    span = TOKEN_BLOCK + 2 * WINDOW
    slots_per_kv = group // 2

    @pl.when(t == 0)
    def _():
        lower_all = lax.broadcasted_iota(jnp.int32, (1, v_ref.shape[1]), 1) % LANES < HALF
        v_ext[...] = jnp.where(lower_all, v_ref[...], jnp.ones_like(v_ref))

    def run(keys, values, valid):
        for h in range(B_KV_HEADS):
            kk, vv = keys(h), values(h)
            for sl in range(h * slots_per_kv, (h + 1) * slots_per_kv):
                qs = q_ref[:, sl * LANES:(sl + 1) * LANES]
                outs = []
                for j in range(2):
                    qm = jnp.where(first_head if j == 0 else jnp.logical_not(first_head), qs, jnp.zeros_like(qs))
                    s = _nt_dot(qm, kk)
                    if valid is not None:
                        s = jnp.concatenate([s[:, :ctx], jnp.where(valid, s[:, ctx:], NEG_INF)], axis=1)
                    snk = sink_ref[sl * 2 + j] * LOG2_E
                    m = jnp.maximum(jnp.max(s, axis=-1, keepdims=True), snk)
                    oe = _dot(jnp.exp2(s - m).astype(BF16), vv)
                    outs.append((oe, pltpu.roll(oe, HALF, 1), jnp.exp2(snk - m)))
                (oa, ra, za), (ob, rb, zb) = outs
                num = jnp.where(lower, oa, rb)
                den = jnp.where(lower, ra + za, ob + zb)
                o_ref[:, sl * LANES:(sl + 1) * LANES] = (num / den).astype(BF16)

    def head(ref, h, rows):
        return ref[rows, h * LANES:(h + 1) * LANES]

    @pl.when(t == 0)
    def _():
        rows = slice(0, ctx)
        run(lambda h: head(k_ref, h, rows), lambda h: head(v_ext, h, rows), None)

    @pl.when(t > 0)
    def _():
        p0 = (t - 1) * TOKEN_BLOCK
        start = pl.multiple_of(jnp.clip(p0 - WINDOW, 0, seq - span), WINDOW)

        def ctx_and_window(ref, h):
            return jnp.concatenate([head(ref, h, slice(0, ctx)), head(ref, h, pl.ds(ctx + start, span))], axis=0)

        dist = (lax.broadcasted_iota(jnp.int32, (TOKEN_BLOCK, span), 1)
                - lax.broadcasted_iota(jnp.int32, (TOKEN_BLOCK, span), 0) + (start - p0))
        run(lambda h: ctx_and_window(k_ref, h), lambda h: ctx_and_window(v_ext, h), jnp.abs(dist) <= WINDOW)


def _attn_b(sink, q, k, v, ctx):
    bsz, t_len, _ = q.shape
    group = B_HEADS // B_KV_HEADS
    qo_spec = pl.BlockSpec((None, TOKEN_BLOCK, q.shape[-1]), lambda b, t, s: (b, t, 0))
    kv_spec = pl.BlockSpec((None, t_len, k.shape[-1]), lambda b, t, s: (b, 0, 0))
    return pl.pallas_call(
        functools.partial(_attn_b_kernel, ctx=ctx, seq=t_len - ctx, group=group),
        grid_spec=pltpu.PrefetchScalarGridSpec(
            num_scalar_prefetch=1,
            grid=(bsz, t_len // TOKEN_BLOCK),
            in_specs=[qo_spec, kv_spec, kv_spec],
            out_specs=qo_spec,
            scratch_shapes=[pltpu.VMEM((t_len, k.shape[-1]), BF16)],
        ),
        out_shape=jax.ShapeDtypeStruct(q.shape, BF16),
        compiler_params=_params(2),
        name="attn_window_sink",
    )(sink, q, k, v)


def _attn_c_kernel(lam_ref, gain_ref, q_ref, k_ref, v_ref, o_ref, *, ctx, lam_init):
    t = pl.program_id(2)
    lane = lax.broadcasted_iota(jnp.int32, (1, LANES), 1)
    first_map = (lane // (HALF // 2)) % 2 == 0
    lam = (jnp.exp(jnp.sum(lam_ref[0:1, :] * lam_ref[1:2, :], axis=-1, keepdims=True))
           - jnp.exp(jnp.sum(lam_ref[2:3, :] * lam_ref[3:4, :], axis=-1, keepdims=True)) + lam_init)

    def attend(n_keys):
        for h in range(q_ref.shape[1] // LANES):
            head = slice(h * LANES, (h + 1) * LANES)
            k, v = k_ref[0:n_keys, head], v_ref[0:n_keys, head]
            q = q_ref[:, head]
            zero = jnp.zeros_like(q)
            s0 = _nt_dot(jnp.where(first_map, q, zero), k)
            s1 = _nt_dot(jnp.where(first_map, zero, q), k)
            e0 = jnp.exp2(s0 - jnp.max(s0, axis=-1, keepdims=True))
            e1 = jnp.exp2(s1 - jnp.max(s1, axis=-1, keepdims=True))
            r0 = 1.0 / jnp.sum(e0, axis=-1, keepdims=True)
            r1 = lam / jnp.sum(e1, axis=-1, keepdims=True)
            o = _dot((e0 * r0 - e1 * r1).astype(BF16), v)
            y = o * lax.rsqrt(jnp.mean(o * o, axis=-1, keepdims=True) + RMS_EPS) * gain_ref[...]
            o_ref[:, head] = (y * (1.0 - lam_init)).astype(BF16)

    pl.when(t == 0)(lambda: attend(ctx))
    pl.when(t > 0)(lambda: attend(k_ref.shape[0]))


def _attn_c(lam_vecs, gain, q, k, v, ctx, lam_init):
    bsz, t_len, _ = q.shape
    width = C_HEADS_PER_STEP * LANES
    qo_spec = pl.BlockSpec((None, TOKEN_BLOCK, width), lambda b, h, t: (b, t, h))
    kv_spec = pl.BlockSpec((None, t_len, width), lambda b, h, t: (b, 0, h))
    return pl.pallas_call(
        functools.partial(_attn_c_kernel, ctx=ctx, lam_init=lam_init),
        grid=(bsz, C_HEADS // C_HEADS_PER_STEP, t_len // TOKEN_BLOCK),
        in_specs=[pl.BlockSpec(lam_vecs.shape, lambda b, h, t: (0, 0)),
                  pl.BlockSpec(gain.shape, lambda b, h, t: (0, 0)),
                  qo_spec, kv_spec, kv_spec],
        out_specs=qo_spec,
        out_shape=jax.ShapeDtypeStruct(q.shape, BF16),
        compiler_params=_params(3),
        name="attn_differential",
    )(lam_vecs, gain, q, k, v)


def _store_token_tiles(ref, x):
    rows, d = x.shape
    rpt = d // LANES
    for c in range(rpt):
        ref[pl.ds(c, rows, stride=rpt), :] = x[:, c * LANES:(c + 1) * LANES]


def _load_token_tiles(ref, rows):
    rpt = ref.shape[0] // rows
    return jnp.concatenate([ref[pl.ds(c, rows, stride=rpt), :] for c in range(rpt)], axis=1)


def _layer_norm(z, gain, bias):
    mu = jnp.mean(z, axis=-1, keepdims=True)
    zc = z - mu
    var = jnp.mean(zc * zc, axis=-1, keepdims=True)
    return zc * lax.rsqrt(var + LN_EPS) * gain + bias


def _oproj_router_kernel(o_ref, o_next_ref, wo_ref, x_ref, mod_ref, lng_ref, lnb_ref, wr_ref, br_ref,
                         xo_ref, h2_ref, ri_ref, wcol_ref, cnt_ref, carry_ref, y_ref, *, alpha):
    step = pl.program_id(0) * pl.num_programs(1) + pl.program_id(1)
    slot = step % 2

    @pl.when(step == 0)
    def _():
        carry_ref[...] = jnp.zeros_like(carry_ref)
        y_ref[0] = _dot(o_ref[...], wo_ref[...])

    tb = x_ref.shape[0]
    y = y_ref[slot]
    y_ref[1 - slot] = _dot(o_next_ref[...], wo_ref[...])
    xn = _layer_norm(alpha * x_ref[...] + mod_ref[2:3, :] * y, lng_ref[0:1, :], lnb_ref[0:1, :])
    xo_ref[...] = xn
    h2 = xn * (1.0 + mod_ref[4:5, :]) + mod_ref[3:4, :]
    h_hi, h_lo = _split_bf16(h2)
    _store_token_tiles(h2_ref, h_hi.astype(F32))
    w_hi, w_lo = _split_bf16(wr_ref[...])
    logits = _nt_dot(w_hi, h_hi) + _nt_dot(w_hi, h_lo) + _nt_dot(w_lo, h_hi) + br_ref[...]

    row8 = lax.broadcasted_iota(jnp.int32, (SUBLANES, tb), 0)
    glog = jnp.where(row8 < N_GROUPS, logits[0:SUBLANES, :], NEG_INF)
    gmax = jnp.max(glog, axis=0, keepdims=True)
    gsel = jnp.min(jnp.where(glog == gmax, row8, SUBLANES), axis=0, keepdims=True)
    gprob = 1.0 / jnp.sum(jnp.exp(glog - gmax), axis=0, keepdims=True)
    elog = jnp.zeros((SUBLANES, tb), F32)
    for g in range(N_GROUPS):
        elog = jnp.where(gsel == g, logits[(g + 1) * SUBLANES:(g + 2) * SUBLANES, :], elog)
    v1 = jnp.max(elog, axis=0, keepdims=True)
    i1 = jnp.min(jnp.where(elog == v1, row8, SUBLANES), axis=0, keepdims=True)
    elog2 = jnp.where(row8 == i1, NEG_INF, elog)
    v2 = jnp.max(elog2, axis=0, keepdims=True)
    i2 = jnp.min(jnp.where(elog2 == v2, row8, SUBLANES), axis=0, keepdims=True)
    t21 = jnp.exp(v2 - v1)
    w1 = gprob / (1.0 + t21)
    w2 = gprob * t21 / (1.0 + t21)
    e1 = gsel * EXPERTS_PER_GROUP + i1
    e2 = gsel * EXPERTS_PER_GROUP + i2

    rowe = lax.broadcasted_iota(jnp.int32, (N_EXPERTS, tb), 0)
    oh1, oh2 = rowe == e1, rowe == e2
    oh = jnp.where(jnp.logical_or(oh1, oh2), 1.0, 0.0)
    before = (lax.broadcasted_iota(jnp.int32, (tb, tb), 0) < lax.broadcasted_iota(jnp.int32, (tb, tb), 1))
    base = _dot(oh.astype(BF16), jnp.where(before, 1.0, 0.0).astype(BF16)) + carry_ref[:, 0:1]
    rank1 = jnp.sum(jnp.where(oh1, base, 0.0), axis=0, keepdims=True).astype(jnp.int32)
    rank2 = jnp.sum(jnp.where(oh2, base, 0.0), axis=0, keepdims=True).astype(jnp.int32)
    carry_ref[...] = carry_ref[...] + jnp.sum(oh, axis=1, keepdims=True)
    cnt_ref[...] = carry_ref[...]

    ri_ref[...] = jnp.where(row8 == 0, e1, jnp.where(row8 == 1, e2, jnp.where(row8 == 2, rank1,
                            jnp.where(row8 == 3, rank2, 0))))
    rowl = lax.broadcasted_iota(jnp.int32, (LANES, tb), 0)
    wcol_ref[...] = jnp.where(rowl == 0, w1, jnp.where(rowl == 1, w2, 0.0)).T


def _oproj_router(o, w_o, x, mod, lng, lnb, w_router, b_router, alpha, first_block):
    bsz, t_in, d = x.shape
    n_t = t_in // TOKEN_BLOCK - first_block
    t_len = n_t * TOKEN_BLOCK
    n_blk = bsz * n_t
    tok_in = lambda b, t: (b, t + first_block, 0)
    tok = lambda b, t: (b, t, 0)
    blk = lambda b, t: (b * n_t + t, 0, 0)
    const2 = lambda b, t: (0, 0)

    def tok_next(b, t):
        s1 = jnp.minimum(b * n_t + t + 1, n_blk - 1)
        return (s1 // n_t, s1 % n_t + first_block, 0)

    return pl.pallas_call(
        functools.partial(_oproj_router_kernel, alpha=alpha),
        grid=(bsz, n_t),
        in_specs=[
            pl.BlockSpec((None, TOKEN_BLOCK, o.shape[-1]), tok_in),
            pl.BlockSpec((None, TOKEN_BLOCK, o.shape[-1]), tok_next),
            pl.BlockSpec(w_o.shape, const2),
            pl.BlockSpec((None, TOKEN_BLOCK, d), tok_in),
            pl.BlockSpec((None, 6, d), lambda b, t: (jnp.where(t + first_block == 0, bsz, b), 0, 0)),
            pl.BlockSpec(lng.shape, const2),
            pl.BlockSpec(lnb.shape, const2),
            pl.BlockSpec(w_router.shape, const2),
            pl.BlockSpec(b_router.shape, const2),
        ],
        out_specs=[
            pl.BlockSpec((None, TOKEN_BLOCK, d), tok),
            pl.BlockSpec((TOKEN_BLOCK * d // LANES, LANES), lambda b, t: (b * n_t + t, 0)),
            pl.BlockSpec((None, SUBLANES, TOKEN_BLOCK), blk),
            pl.BlockSpec((None, TOKEN_BLOCK, LANES), blk),
            pl.BlockSpec((N_EXPERTS, LANES), const2),
        ],
        out_shape=[
            jax.ShapeDtypeStruct((bsz, t_len, d), F32),
            jax.ShapeDtypeStruct((bsz * t_len * d // LANES, LANES), F32),
            jax.ShapeDtypeStruct((n_blk, SUBLANES, TOKEN_BLOCK), jnp.int32),
            jax.ShapeDtypeStruct((n_blk, TOKEN_BLOCK, LANES), F32),
            jax.ShapeDtypeStruct((N_EXPERTS, LANES), F32),
        ],
        scratch_shapes=[pltpu.VMEM((N_EXPERTS, LANES), F32), pltpu.VMEM((2, TOKEN_BLOCK, d), F32)],
        compiler_params=_params(2),
        name="oproj_ln_router",
    )(o, o, w_o, x, mod, lng, lnb, w_router, b_router)


def _for_each_row(n_rows, fn):
    def chunk(c, carry):
        for u in range(ROW_UNROLL):
            fn(c * ROW_UNROLL + u)
        return carry

    lax.fori_loop(0, n_rows // ROW_UNROLL, chunk, 0)


def _expert_row(dest_ref, block, k, r):
    return dest_ref[(block * 2 + k) * TOKEN_BLOCK + r]


def _token_rows(ref, row, n, rpt):
    return ref.at[pl.ds(pl.multiple_of(row * rpt, rpt), n * rpt)]


def _dispatch_kernel(off_ref, tiles_ref, dest_ref, h_ref, xs_ref, zeros, sems, *, rpt):
    tb = h_ref.shape[0] // rpt
    block = pl.program_id(0)

    @pl.when(pl.program_id(0) == 0)
    def _():
        zeros[...] = jnp.zeros_like(zeros)

        def last_tile(e):
            return pltpu.make_async_copy(
                zeros, _token_rows(xs_ref, off_ref[e] + (tiles_ref[e] - 1) * EXPERT_TILE, EXPERT_TILE, rpt), sems.at[2])

        for e in range(N_EXPERTS):
            pl.when(tiles_ref[e] > 0)(lambda e=e: last_tile(e).start())
        for e in range(N_EXPERTS):
            pl.when(tiles_ref[e] > 0)(lambda e=e: last_tile(e).wait())

        def spare_tile(i):
            return pltpu.make_async_copy(zeros, _token_rows(xs_ref, i * EXPERT_TILE, EXPERT_TILE, rpt), sems.at[2])

        first_spare = off_ref[N_EXPERTS - 1] // EXPERT_TILE + tiles_ref[N_EXPERTS - 1]
        n_tiles = xs_ref.shape[0] // (EXPERT_TILE * rpt)
        lax.fori_loop(first_spare, n_tiles, lambda i, c: (spare_tile(i).start(), c)[1], 0)
        lax.fori_loop(first_spare, n_tiles, lambda i, c: (spare_tile(i).wait(), c)[1], 0)

    for sub in range(tb // TOKEN_BLOCK):
        def start(r, sub=sub):
            for k in range(2):
                pltpu.make_async_copy(
                    _token_rows(h_ref, sub * TOKEN_BLOCK + r, 1, rpt),
                    _token_rows(xs_ref, _expert_row(dest_ref, block * (tb // TOKEN_BLOCK) + sub, k, r), 1, rpt),
                    sems.at[k]).start(priority=k)

        _for_each_row(TOKEN_BLOCK, start)
    for k in range(2):
        pltpu.make_async_copy(h_ref, _token_rows(xs_ref, 0, tb, rpt), sems.at[k]).wait()


def _dispatch(row_off, tiles, dest, h2, n_rows, rpt):
    n_blk = h2.shape[0] // (TOKEN_BLOCK * rpt)
    per_step = 2 if n_blk % 2 == 0 else 1
    return pl.pallas_call(
        functools.partial(_dispatch_kernel, rpt=rpt),
        grid_spec=pltpu.PrefetchScalarGridSpec(
            num_scalar_prefetch=3,
            grid=(n_blk // per_step,),
            in_specs=[pl.BlockSpec((per_step * TOKEN_BLOCK * rpt, LANES), lambda i, off, nt, dst: (i, 0))],
            out_specs=pl.BlockSpec(memory_space=pl.ANY),
            scratch_shapes=[pltpu.VMEM((EXPERT_TILE * rpt, LANES), F32), pltpu.SemaphoreType.DMA((3,))],
        ),
        out_shape=jax.ShapeDtypeStruct((n_rows * rpt, LANES), F32),
        compiler_params=_params(1),
        name="moe_dispatch",
    )(row_off, tiles, dest, h2)


def _experts_kernel(te_ref, nu_ref, nxt_ref, xs_ref, wg_hbm, wu_hbm, wd_hbm, ys_ref,
                    wg_f, wu_f, wd_f, wg_s, wu_s, wd_s, slot_ref, sems, *, layer):
    t = pl.program_id(0)

    def fetch(e, slot):
        g, i = e // EXPERTS_PER_GROUP, e % EXPERTS_PER_GROUP
        return [pltpu.make_async_copy(w.at[layer, g, i], buf.at[slot], sems.at[slot, n])
                for n, (w, buf) in enumerate(((wg_hbm, wg_f), (wu_hbm, wu_f), (wd_hbm, wd_f)))]

    @pl.when(t < nu_ref[0])
    def _():
        e = te_ref[t]

        @pl.when(t == 0)
        def _():
            slot_ref[0] = 0
            for c in fetch(e, 0):
                c.start()

        @pl.when(jnp.logical_or(t == 0, e != te_ref[jnp.maximum(t - 1, 0)]))
        def _():
            slot = slot_ref[0]
            for c in fetch(e, slot):
                c.wait()
            nxt = nxt_ref[e]

            @pl.when(nxt >= 0)
            def _():
                for c in fetch(nxt, 1 - slot):
                    c.start()

            wg_s[...] = wg_f[slot].astype(BF16)
            wu_s[...] = wu_f[slot].astype(BF16)
            wd_s[...] = wd_f[slot].astype(BF16)
            slot_ref[0] = 1 - slot

        x = _load_token_tiles(xs_ref, EXPERT_TILE).astype(BF16)
        a = _dot(x, wg_s[...])
        u = _dot(x, wu_s[...])
        hid = a * (1.0 / (1.0 + jnp.exp(-a))) * u
        _store_token_tiles(ys_ref, _dot(hid.astype(BF16), wd_s[...]))

    @pl.when(t >= nu_ref[0])
    def _():
        ys_ref[...] = jnp.zeros_like(ys_ref)


def _experts(tile_expert, n_used, next_expert, xs, w_gate, w_up, w_down, layer):
    d, ff = w_gate.shape[-2:]
    tile_rows = EXPERT_TILE * d // LANES
    n_tiles = xs.shape[0] // tile_rows
    hbm = pl.BlockSpec(memory_space=pl.ANY)
    return pl.pallas_call(
        functools.partial(_experts_kernel, layer=layer),
        grid_spec=pltpu.PrefetchScalarGridSpec(
            num_scalar_prefetch=3,
            grid=(n_tiles,),
            in_specs=[
                pl.BlockSpec((tile_rows, LANES), lambda t, te, nu, nx: (jnp.minimum(t, nu[0] - 1), 0)),
                hbm, hbm, hbm,
            ],
            out_specs=pl.BlockSpec((tile_rows, LANES), lambda t, te, nu, nx: (t, 0)),
            scratch_shapes=[
                pltpu.VMEM((2, d, ff), F32), pltpu.VMEM((2, d, ff), F32), pltpu.VMEM((2, ff, d), F32),
                pltpu.VMEM((d, ff), BF16), pltpu.VMEM((d, ff), BF16), pltpu.VMEM((ff, d), BF16),
                pltpu.SMEM((1,), jnp.int32), pltpu.SemaphoreType.DMA((2, 3)),
            ],
        ),
        out_shape=jax.ShapeDtypeStruct(xs.shape, F32),
        compiler_params=_params(1),
        name="moe_experts",
    )(tile_expert, n_used, next_expert, xs, w_gate, w_up, w_down)


def _combine_kernel(dest_ref, ys_ref, wcol_ref, x_ref, mod_ref, lng_ref, lnb_ref, o_ref, buf, sems, *, alpha):
    tb, d = x_ref.shape
    rpt = d // LANES
    step = pl.program_id(0) * pl.num_programs(1) + pl.program_id(1)
    n_steps = pl.num_programs(0) * pl.num_programs(1)
    slot = step % 2

    def gather(block, into):
        def start(r):
            for k in range(2):
                pltpu.make_async_copy(_token_rows(ys_ref, _expert_row(dest_ref, block, k, r), 1, rpt),
                                      _token_rows(buf.at[into, k], r, 1, rpt), sems.at[into, k]).start(priority=k)

        _for_each_row(tb, start)

    pl.when(step == 0)(lambda: gather(step, 0))
    pl.when(step + 1 < n_steps)(lambda: gather(step + 1, 1 - slot))
    for k in range(2):
        pltpu.make_async_copy(_token_rows(ys_ref, 0, tb, rpt), buf.at[slot, k], sems.at[slot, k]).wait()
    f = (wcol_ref[:, 0:1] * _load_token_tiles(buf.at[slot, 0], tb)
         + wcol_ref[:, 1:2] * _load_token_tiles(buf.at[slot, 1], tb))
    o_ref[...] = _layer_norm(alpha * x_ref[...] + mod_ref[5:6, :] * f, lng_ref[1:2, :], lnb_ref[1:2, :])


def _combine(dest, ys, wcol, x, mod, lng, lnb, alpha, has_ctx):
    bsz, t_len, d = x.shape
    n_t = t_len // TOKEN_BLOCK
    tok = lambda b, t, off: (b, t, 0)
    blk = lambda b, t, off: (b * n_t + t, 0, 0)
    const2 = lambda b, t, off: (0, 0)
    mod_row = (lambda b, t, off: (jnp.where(t == 0, bsz, b), 0, 0)) if has_ctx else (lambda b, t, off: (b, 0, 0))
    return pl.pallas_call(
        functools.partial(_combine_kernel, alpha=alpha),
        grid_spec=pltpu.PrefetchScalarGridSpec(
            num_scalar_prefetch=1,
            grid=(bsz, n_t),
            in_specs=[
                pl.BlockSpec(memory_space=pl.ANY),
                pl.BlockSpec((None, TOKEN_BLOCK, LANES), blk),
                pl.BlockSpec((None, TOKEN_BLOCK, d), tok),
                pl.BlockSpec((None, 6, d), mod_row),
                pl.BlockSpec(lng.shape, const2),
                pl.BlockSpec(lnb.shape, const2),
            ],
            out_specs=pl.BlockSpec((None, TOKEN_BLOCK, d), tok),
            scratch_shapes=[pltpu.VMEM((2, 2, TOKEN_BLOCK * d // LANES, LANES), F32),
                            pltpu.SemaphoreType.DMA((2, 2))],
        ),
        out_shape=jax.ShapeDtypeStruct(x.shape, F32),
        compiler_params=_params(2),
        name="moe_combine_ln",
    )(dest, ys, wcol, x, mod, lng, lnb)


def _qkv_columns(kind):
    if kind == 0:
        perm, *_ = _slot_layout(A_HEAD_DIM // 4)
        nq, nk = A_HEADS, A_KV_HEADS
        q = np.concatenate([s * LANES + perm for s in range(nq)])
        k = np.concatenate([nq * LANES + s * LANES + perm for s in range(nk)])
        v = np.arange((nq + nk) * LANES, (nq + 2 * nk) * LANES)
        return np.concatenate([q, k, v]), nq, nk, nk
    perm, axis_of, freq_of, half_of, _ = _slot_layout(B_HEAD_DIM // 4)
    nf = B_HEAD_DIM // 4
    if kind == 1:
        nq = B_HEADS // 2
        q = np.concatenate([s * LANES + perm for s in range(nq)])
        k_base = B_HEADS * B_HEAD_DIM
        v_base = k_base + B_KV_HEADS * B_HEAD_DIM
        in_head = axis_of * 2 * nf + half_of * nf + freq_of
        k = np.concatenate([k_base + h * B_HEAD_DIM + in_head for h in range(B_KV_HEADS)])
        v = np.concatenate([v_base + h * B_HEAD_DIM + np.arange(LANES) % B_HEAD_DIM for h in range(B_KV_HEADS)])
        return np.concatenate([q, k, v]), nq, B_KV_HEADS, B_KV_HEADS
    nq = C_HEADS
    q = np.concatenate([s * LANES + perm for s in range(nq)])
    k = np.concatenate([nq * LANES + s * LANES + perm for s in range(nq)])
    v = np.arange(2 * nq * LANES, 3 * nq * LANES)
    return np.concatenate([q, k, v]), nq, nq, nq


def kernel(x, c, ctx, c_ctx, w_mod, b_mod, ln_gain, ln_bias, a_w_qkv, a_q_gain, a_k_gain, a_w_o, b_w_qkv, b_sink, b_w_o, c_w_qkv, c_lam_q1, c_lam_k1, c_lam_q2, c_lam_k2, c_subln_gain, c_w_o, moe_w_group, moe_b_group, moe_w_expert, moe_b_expert, moe_w_gate, moe_w_up, moe_w_down):
    bsz, seq, d = x.shape
    n_ctx = ctx.shape[1]
    depth = w_mod.shape[0]
    assert n_ctx == TOKEN_BLOCK and seq % TOKEN_BLOCK == 0 and seq >= TOKEN_BLOCK + 2 * WINDOW
    assert d % (SUBLANES * LANES) == 0
    t_len = n_ctx + seq
    n_tok = bsz * t_len
    alpha = (2.0 * depth) ** 0.25

    mod_rows = -(-(bsz + 1) // SUBLANES) * SUBLANES
    cc = jnp.concatenate([c, c_ctx[None, :], jnp.zeros((mod_rows - bsz - 1, d), F32)], axis=0)
    mod_all = _modulation(cc, w_mod, b_mod).reshape(depth, mod_rows, 6, d)

    stream = jnp.concatenate([ctx, x], axis=1)
    tables = {hd: _rope_tables(seq, n_ctx, hd // 4, hd ** -0.5 * LOG2_E) for hd in (A_HEAD_DIM, B_HEAD_DIM)}
    perm_a = _slot_layout(A_HEAD_DIM // 4)[0]

    for i in range(depth):
        kind, j = i % N_MIXERS, i // N_MIXERS
        mod = mod_all[i]
        cols, n_q, n_k, n_v = _qkv_columns(kind)
        if kind == 0:
            w = a_w_qkv[j][:, cols].astype(BF16)
            gains = (a_q_gain[j][perm_a][None, :], a_k_gain[j][perm_a][None, :])
            q, k, v = _qkv_call(stream, mod, w, tables[A_HEAD_DIM], gains, n_q, n_k, n_v)
            o = _attn_a(q, k, v, n_ctx)
            w_o = a_w_o[j]
        elif kind == 1:
            w = b_w_qkv[j][:, cols].astype(BF16)
            q, k, v = _qkv_call(stream, mod, w, tables[B_HEAD_DIM], None, n_q, n_k, n_v)
            o = _attn_b(b_sink[j], q, k, v, n_ctx)
            w_o = b_w_o[j]
        else:
            w = c_w_qkv[j][:, cols].astype(BF16)
            q, k, v = _qkv_call(stream, mod, w, tables[C_HEAD_DIM], None, n_q, n_k, n_v)
            lam_init = 0.8 - 0.6 * float(np.exp(-0.3 * i))
            lam_vecs = jnp.stack([c_lam_q1[j], c_lam_k1[j], c_lam_q2[j], c_lam_k2[j]], axis=0)
            o = _attn_c(lam_vecs, c_subln_gain[j][None, :], q, k, v, n_ctx, lam_init)
            w_o = c_w_o[j]

        w_router = jnp.zeros((ROUTER_ROWS, d), F32)
        w_router = w_router.at[0:N_GROUPS].set(moe_w_group[i].T).at[SUBLANES:SUBLANES + N_EXPERTS].set(moe_w_expert[i].T)
        b_router = jnp.zeros((ROUTER_ROWS, 1), F32)
        b_router = b_router.at[0:N_GROUPS, 0].set(moe_b_group[i]).at[SUBLANES:SUBLANES + N_EXPERTS, 0].set(moe_b_expert[i])

        last = i == depth - 1
        stream, h2, rinfo, wcol, counts = _oproj_router(
            o, w_o.astype(BF16), stream, mod, ln_gain[i], ln_bias[i], w_router, b_router, alpha, 1 if last else 0)

        n_tiles = -(-2 * rinfo.shape[0] * TOKEN_BLOCK // EXPERT_TILE) + N_EXPERTS
        xs_rows = n_tiles * EXPERT_TILE
        cnt = counts[:, 0].astype(jnp.int32)
        tiles = (cnt + EXPERT_TILE - 1) // EXPERT_TILE
        tile_end = jnp.cumsum(tiles)
        row_off = (tile_end - tiles) * EXPERT_TILE
        n_used = tile_end[-1:].astype(jnp.int32)
        tile_ids = jnp.arange(n_tiles, dtype=jnp.int32)
        tile_expert = jnp.minimum(
            jnp.sum((tile_end[None, :] <= tile_ids[:, None]).astype(jnp.int32), axis=1), N_EXPERTS - 1)

        ids = jnp.arange(N_EXPERTS, dtype=jnp.int32)
        later_used = jnp.logical_and(ids[None, :] > ids[:, None], tiles[None, :] > 0)
        next_expert = jnp.min(jnp.where(later_used, ids[None, :], N_EXPERTS), axis=1)
        next_expert = jnp.where(next_expert == N_EXPERTS, -1, next_expert).astype(jnp.int32)

        expert_start = jnp.sum(jnp.where(rinfo[:, 0:2, :, None] == ids, row_off, 0), axis=-1)
        dest = (expert_start + rinfo[:, 2:4, :]).reshape(-1)

        xs = _dispatch(row_off, tiles, dest, h2, xs_rows, d // LANES)
        ys = _experts(tile_expert, n_used, next_expert, xs, moe_w_gate, moe_w_up, moe_w_down, i)
        stream = _combine(dest, ys, wcol, stream, mod, ln_gain[i], ln_bias[i], alpha, not last)

    return stream
```

```python
import functools

import numpy as np
import jax
import jax.numpy as jnp
from jax import lax
from jax.experimental import pallas as pl
from jax.experimental.pallas import tpu as pltpu

GRID_W = 64
ROPE_THETA = 10000.0
LN_EPS = 1e-5
RMS_EPS = 1e-6
NEG_INF = -1e30
N_MIXERS = 3
LOG2_E = float(np.log2(np.e))

A_HEADS, A_KV_HEADS, A_HEAD_DIM = 8, 2, 128
B_HEADS, B_KV_HEADS, B_HEAD_DIM = 16, 4, 64
WINDOW = 128
C_HEADS, C_HEAD_DIM = 8, 64
N_GROUPS, EXPERTS_PER_GROUP = 4, 8
N_EXPERTS = N_GROUPS * EXPERTS_PER_GROUP

LANES = 128
SUBLANES = 8
HALF = LANES // 2
TOKEN_BLOCK = 256
EXPERT_TILE = 512
ROUTER_ROWS = 128
ROW_UNROLL = 8
C_HEADS_PER_STEP = 4
VMEM_LIMIT = 56 * 1024 * 1024

F32 = jnp.float32
BF16 = jnp.bfloat16


def _nt_dot(a, b):
    return lax.dot_general(a, b, (((1,), (1,)), ((), ())), preferred_element_type=F32)


def _dot(a, b):
    return jnp.dot(a, b, preferred_element_type=F32)


def _split_bf16(x):
    hi = x.astype(BF16)
    lo = (x - hi.astype(F32)).astype(BF16)
    return hi, lo


def _params(n_axes):
    return pltpu.CompilerParams(dimension_semantics=("arbitrary",) * n_axes, vmem_limit_bytes=VMEM_LIMIT)


def _slot_layout(n_freq):
    n_sub = LANES // (4 * n_freq)
    perm = np.zeros(LANES, np.int32)
    axis_of = np.zeros(LANES, np.int32)
    freq_of = np.zeros(LANES, np.int32)
    half_of = np.zeros(LANES, np.int32)
    sub_of = np.zeros(LANES, np.int32)
    for half in range(2):
        for sub in range(n_sub):
            for axis in range(2):
                for f in range(n_freq):
                    l = half * HALF + sub * 2 * n_freq + axis * n_freq + f
                    perm[l] = sub * 4 * n_freq + axis * 2 * n_freq + half * n_freq + f
                    axis_of[l], freq_of[l], half_of[l], sub_of[l] = axis, f, half, sub
    return perm, axis_of, freq_of, half_of, sub_of


def _rope_tables(seq, ctx, n_freq, q_scale):
    _, axis_of, freq_of, half_of, _ = _slot_layout(n_freq)
    rows = seq // GRID_W
    row = jnp.broadcast_to(jnp.arange(rows, dtype=F32)[:, None], (rows, GRID_W)).reshape(-1)
    col = jnp.broadcast_to(jnp.arange(GRID_W, dtype=F32)[None, :], (rows, GRID_W)).reshape(-1)
    inv = ROPE_THETA ** (-jnp.arange(n_freq, dtype=F32) / n_freq)
    ang = jnp.stack([row[:, None] * inv, col[:, None] * inv], axis=1)
    cos, sin = jnp.cos(ang), jnp.sin(ang)
    cos_l = cos[:, axis_of, freq_of]
    sin_l = sin[:, axis_of, freq_of] * jnp.asarray(np.where(half_of == 0, -1.0, 1.0), F32)
    cos_l = jnp.concatenate([jnp.ones((ctx, LANES), F32), cos_l], axis=0)
    sin_l = jnp.concatenate([jnp.zeros((ctx, LANES), F32), sin_l], axis=0)
    return cos_l * q_scale, sin_l * q_scale, cos_l, sin_l


def _mod_kernel(c_ref, w_ref, b_ref, o_ref):
    c = c_ref[...]
    s = c * (1.0 / (1.0 + jnp.exp(-c)))
    s_hi, s_lo = _split_bf16(s)
    w_hi, w_lo = _split_bf16(w_ref[...])
    o_ref[...] = _dot(s_hi, w_hi) + _dot(s_lo, w_hi) + _dot(s_hi, w_lo) + b_ref[...]


def _modulation(cc, w_mod, b_mod):
    depth, d, n = w_mod.shape
    rows = cc.shape[0]
    tn = n // 4
    return pl.pallas_call(
        _mod_kernel,
        grid=(depth, n // tn),
        in_specs=[
            pl.BlockSpec((rows, d), lambda l, j: (0, 0)),
            pl.BlockSpec((None, d, tn), lambda l, j: (l, 0, j)),
            pl.BlockSpec((None, 1, tn), lambda l, j: (l, 0, j)),
        ],
        out_specs=pl.BlockSpec((None, rows, tn), lambda l, j: (l, 0, j)),
        out_shape=jax.ShapeDtypeStruct((depth, rows, n), F32),
        compiler_params=_params(2),
        name="modulation",
    )(cc, w_mod, b_mod.reshape(depth, 1, n))


def _stream_block(refs, split):
    if not split:
        return refs.pop(0)[...]
    ctx_ref, lat_ref = refs.pop(0), refs.pop(0)
    return jnp.where(pl.program_id(1) == 0, ctx_ref[...], lat_ref[...])


def _stream_specs(x, d):
    if not isinstance(x, tuple):
        return [pl.BlockSpec((None, TOKEN_BLOCK, d), lambda b, t, *_: (b, t, 0))], [x]
    return ([pl.BlockSpec((None, TOKEN_BLOCK, d), lambda b, t, *_: (b, 0, 0)),
             pl.BlockSpec((None, TOKEN_BLOCK, d), lambda b, t, *_: (b, jnp.maximum(t - 1, 0), 0))], list(x))


def _qkv_kernel(*refs, n_q, n_k, rms, split):
    refs = list(refs)
    x = _stream_block(refs, split)
    if rms:
        mod_ref, w_ref, cq_ref, sq_ref, ck_ref, sk_ref, gq_ref, gk_ref, q_ref, k_ref, v_ref = refs
    else:
        mod_ref, w_ref, cq_ref, sq_ref, ck_ref, sk_ref, q_ref, k_ref, v_ref = refs
        gq_ref = gk_ref = None
    sh, sc = mod_ref[0:1, :], mod_ref[1:2, :]
    h = (x * (1.0 + sc) + sh).astype(BF16)

    def rope_slots(y, cos, sin, gain_ref, out_ref, n_slots):
        for j in range(n_slots):
            yj = y[:, j * LANES:(j + 1) * LANES]
            if gain_ref is not None:
                ms = jnp.mean(yj * yj, axis=-1, keepdims=True)
                yj = yj * lax.rsqrt(ms + RMS_EPS) * gain_ref[...]
            out = yj * cos + pltpu.roll(yj, HALF, 1) * sin
            out_ref[:, j * LANES:(j + 1) * LANES] = out.astype(BF16)

    nq, nk = n_q * LANES, n_k * LANES
    rope_slots(_dot(h, w_ref[:, 0:nq]), cq_ref[...], sq_ref[...], gq_ref, q_ref, n_q)
    rope_slots(_dot(h, w_ref[:, nq:nq + nk]), ck_ref[...], sk_ref[...], gk_ref, k_ref, n_k)
    v_ref[...] = _dot(h, w_ref[:, nq + nk:]).astype(BF16)


def _qkv_call(x, mod, w, tables, gains, n_q, n_k, n_v):
    split = isinstance(x, tuple)
    bsz, d = (x[0] if split else x).shape[0::2]
    t_len = x[0].shape[1] + x[1].shape[1] if split else x.shape[1]
    n_t = t_len // TOKEN_BLOCK
    rms = gains is not None
    tok = lambda b, t: (b, t, 0)
    tab_spec = pl.BlockSpec((TOKEN_BLOCK, LANES), lambda b, t: (t, 0))
    x_specs, x_args = _stream_specs(x, d)
    in_specs = x_specs + [
        pl.BlockSpec((None, 6, d), lambda b, t: (jnp.where(t == 0, bsz, b), 0, 0)),
        pl.BlockSpec(w.shape, lambda b, t: (0, 0)),
        tab_spec, tab_spec, tab_spec, tab_spec,
    ]
    args = x_args + [mod, w, *tables]
    if rms:
        in_specs += [pl.BlockSpec((1, LANES), lambda b, t: (0, 0))] * 2
        args += list(gains)
    return pl.pallas_call(
        functools.partial(_qkv_kernel, n_q=n_q, n_k=n_k, rms=rms, split=split),
        grid=(bsz, n_t),
        in_specs=in_specs,
        out_specs=[pl.BlockSpec((None, TOKEN_BLOCK, n * LANES), tok) for n in (n_q, n_k, n_v)],
        out_shape=[jax.ShapeDtypeStruct((bsz, t_len, n * LANES), BF16) for n in (n_q, n_k, n_v)],
        compiler_params=_params(2),
        name="qkv_rope",
    )(*args)


def _softmax_pv(s, v_ext):
    e = jnp.exp2(s - jnp.max(s, axis=-1, keepdims=True))
    oe = _dot(e.astype(BF16), v_ext)
    return oe[:, :LANES], oe[:, LANES:]


def _fill_v_ext(v_ref, v_ext):
    v_ext[:, :LANES] = v_ref[...]
    v_ext[:, LANES:] = jnp.ones((v_ref.shape[0], LANES), BF16)


def _attn_a_kernel(q_ref, k_ref, v_ref, o_ref, v_ext, *, ctx, group):
    t = pl.program_id(2)
    pl.when(t == 0)(lambda: _fill_v_ext(v_ref, v_ext))

    def attend(n_keys):
        k, v = k_ref[0:n_keys, :], v_ext[0:n_keys, :]
        for g in range(group):
            num, den = _softmax_pv(_nt_dot(q_ref[:, g * LANES:(g + 1) * LANES], k), v)
            o_ref[:, g * LANES:(g + 1) * LANES] = (num / den).astype(BF16)

    pl.when(t == 0)(lambda: attend(ctx))
    pl.when(t > 0)(lambda: attend(k_ref.shape[0]))


def _attn_a(q, k, v, ctx):
    bsz, t_len, _ = q.shape
    group = A_HEADS // A_KV_HEADS
    qo_spec = pl.BlockSpec((None, TOKEN_BLOCK, group * LANES), lambda b, h, t: (b, t, h))
    kv_spec = pl.BlockSpec((None, t_len, LANES), lambda b, h, t: (b, 0, h))
    return pl.pallas_call(
        functools.partial(_attn_a_kernel, ctx=ctx, group=group),
        grid=(bsz, A_KV_HEADS, t_len // TOKEN_BLOCK),
        in_specs=[qo_spec, kv_spec, kv_spec],
        out_specs=qo_spec,
        out_shape=jax.ShapeDtypeStruct(q.shape, BF16),
        scratch_shapes=[pltpu.VMEM((t_len, 2 * LANES), BF16)],
        compiler_params=_params(3),
        name="attn_dense_gqa",
    )(q, k, v)


def _attn_b_kernel(sink_ref, q_ref, k_ref, v_ref, o_ref, v_ext, *, ctx, seq, group):
    t = pl.program_id(1)
    lane = lax.broadcasted_iota(jnp.int32, (1, LANES), 1)
    first_head = (lane // (HALF // 2)) % 2 == 0
    lower = lane < HALF
    span = TOKEN_BLOCK + 2 * WINDOW
    slots_per_kv = group // 2

    @pl.when(t == 0)
    def _():
        lower_all = lax.broadcasted_iota(jnp.int32, (1, v_ref.shape[1]), 1) % LANES < HALF
        v_ext[...] = jnp.where(lower_all, v_ref[...], jnp.ones_like(v_ref))

    def run(keys, values, valid):
        for h in range(B_KV_HEADS):
            kk, vv = keys(h), values(h)
            for sl in range(h * slots_per_kv, (h + 1) * slots_per_kv):
                qs = q_ref[:, sl * LANES:(sl + 1) * LANES]
                outs = []
                for j in range(2):
                    qm = jnp.where(first_head if j == 0 else jnp.logical_not(first_head), qs, jnp.zeros_like(qs))
                    s = _nt_dot(qm, kk)
                    if valid is not None:
                        s = jnp.concatenate([s[:, :ctx], jnp.where(valid, s[:, ctx:], NEG_INF)], axis=1)
                    snk = sink_ref[sl * 2 + j] * LOG2_E
                    m = jnp.maximum(jnp.max(s, axis=-1, keepdims=True), snk)
                    oe = _dot(jnp.exp2(s - m).astype(BF16), vv)
                    outs.append((oe, pltpu.roll(oe, HALF, 1), jnp.exp2(snk - m)))
                (oa, ra, za), (ob, rb, zb) = outs
                num = jnp.where(lower, oa, rb)
                den = jnp.where(lower, ra + za, ob + zb)
                o_ref[:, sl * LANES:(sl + 1) * LANES] = (num / den).astype(BF16)

    def head(ref, h, rows):
        return ref[rows, h * LANES:(h + 1) * LANES]

    @pl.when(t == 0)
    def _():
        rows = slice(0, ctx)
        run(lambda h: head(k_ref, h, rows), lambda h: head(v_ext, h, rows), None)

    @pl.when(t > 0)
    def _():
        p0 = (t - 1) * TOKEN_BLOCK
        start = pl.multiple_of(jnp.clip(p0 - WINDOW, 0, seq - span), WINDOW)

        def ctx_and_window(ref, h):
            return jnp.concatenate([head(ref, h, slice(0, ctx)), head(ref, h, pl.ds(ctx + start, span))], axis=0)

        dist = (lax.broadcasted_iota(jnp.int32, (TOKEN_BLOCK, span), 1)
                - lax.broadcasted_iota(jnp.int32, (TOKEN_BLOCK, span), 0) + (start - p0))
        run(lambda h: ctx_and_window(k_ref, h), lambda h: ctx_and_window(v_ext, h), jnp.abs(dist) <= WINDOW)


def _attn_b(sink, q, k, v, ctx):
    bsz, t_len, _ = q.shape
    group = B_HEADS // B_KV_HEADS
    qo_spec = pl.BlockSpec((None, TOKEN_BLOCK, q.shape[-1]), lambda b, t, s: (b, t, 0))
    kv_spec = pl.BlockSpec((None, t_len, k.shape[-1]), lambda b, t, s: (b, 0, 0))
    return pl.pallas_call(
        functools.partial(_attn_b_kernel, ctx=ctx, seq=t_len - ctx, group=group),
        grid_spec=pltpu.PrefetchScalarGridSpec(
            num_scalar_prefetch=1,
            grid=(bsz, t_len // TOKEN_BLOCK),
            in_specs=[qo_spec, kv_spec, kv_spec],
            out_specs=qo_spec,
            scratch_shapes=[pltpu.VMEM((t_len, k.shape[-1]), BF16)],
        ),
        out_shape=jax.ShapeDtypeStruct(q.shape, BF16),
        compiler_params=_params(2),
        name="attn_window_sink",
    )(sink, q, k, v)


def _attn_c_kernel(lam_ref, gain_ref, q_ref, k_ref, v_ref, o_ref, *, ctx, lam_init):
    t = pl.program_id(2)
    lane = lax.broadcasted_iota(jnp.int32, (1, LANES), 1)
    first_map = (lane // (HALF // 2)) % 2 == 0
    lam = (jnp.exp(jnp.sum(lam_ref[0:1, :] * lam_ref[1:2, :], axis=-1, keepdims=True))
           - jnp.exp(jnp.sum(lam_ref[2:3, :] * lam_ref[3:4, :], axis=-1, keepdims=True)) + lam_init)

    def attend(n_keys):
        for h in range(q_ref.shape[1] // LANES):
            head = slice(h * LANES, (h + 1) * LANES)
            k, v = k_ref[0:n_keys, head], v_ref[0:n_keys, head]
            q = q_ref[:, head]
            zero = jnp.zeros_like(q)
            s0 = _nt_dot(jnp.where(first_map, q, zero), k)
            s1 = _nt_dot(jnp.where(first_map, zero, q), k)
            e0 = jnp.exp2(s0 - jnp.max(s0, axis=-1, keepdims=True))
            e1 = jnp.exp2(s1 - jnp.max(s1, axis=-1, keepdims=True))
            r0 = 1.0 / jnp.sum(e0, axis=-1, keepdims=True)
            r1 = lam / jnp.sum(e1, axis=-1, keepdims=True)
            o = _dot((e0 * r0 - e1 * r1).astype(BF16), v)
            y = o * lax.rsqrt(jnp.mean(o * o, axis=-1, keepdims=True) + RMS_EPS) * gain_ref[...]
            o_ref[:, head] = (y * (1.0 - lam_init)).astype(BF16)

    pl.when(t == 0)(lambda: attend(ctx))
    pl.when(t > 0)(lambda: attend(k_ref.shape[0]))


def _attn_c(lam_vecs, gain, q, k, v, ctx, lam_init):
    bsz, t_len, _ = q.shape
    width = C_HEADS_PER_STEP * LANES
    qo_spec = pl.BlockSpec((None, TOKEN_BLOCK, width), lambda b, h, t: (b, t, h))
    kv_spec = pl.BlockSpec((None, t_len, width), lambda b, h, t: (b, 0, h))
    return pl.pallas_call(
        functools.partial(_attn_c_kernel, ctx=ctx, lam_init=lam_init),
        grid=(bsz, C_HEADS // C_HEADS_PER_STEP, t_len // TOKEN_BLOCK),
        in_specs=[pl.BlockSpec(lam_vecs.shape, lambda b, h, t: (0, 0)),
                  pl.BlockSpec(gain.shape, lambda b, h, t: (0, 0)),
                  qo_spec, kv_spec, kv_spec],
        out_specs=qo_spec,
        out_shape=jax.ShapeDtypeStruct(q.shape, BF16),
        compiler_params=_params(3),
        name="attn_differential",
    )(lam_vecs, gain, q, k, v)


def _store_token_tiles(ref, x):
    rows, d = x.shape
    rpt = d // LANES
    for c in range(rpt):
        ref[pl.ds(c, rows, stride=rpt), :] = x[:, c * LANES:(c + 1) * LANES]


def _load_token_tiles(ref, rows):
    rpt = ref.shape[0] // rows
    return jnp.concatenate([ref[pl.ds(c, rows, stride=rpt), :] for c in range(rpt)], axis=1)


def _layer_norm(z, gain, bias):
    mu = jnp.mean(z, axis=-1, keepdims=True)
    zc = z - mu
    var = jnp.mean(zc * zc, axis=-1, keepdims=True)
    return zc * lax.rsqrt(var + LN_EPS) * gain + bias


def _oproj_router_kernel(*refs, alpha, split):
    refs = list(refs)
    x = _stream_block(refs, split)
    (o_ref, o_next_ref, wo_ref, mod_ref, lng_ref, lnb_ref, wr_ref, br_ref,
     xo_ref, h2_ref, ri_ref, wcol_ref, cnt_ref, carry_ref, y_ref) = refs
    step = pl.program_id(0) * pl.num_programs(1) + pl.program_id(1)
    slot = step % 2

    @pl.when(step == 0)
    def _():
        carry_ref[...] = jnp.zeros_like(carry_ref)
        y_ref[0] = _dot(o_ref[...], wo_ref[...])

    tb = x.shape[0]
    y = y_ref[slot]
    y_ref[1 - slot] = _dot(o_next_ref[...], wo_ref[...])
    xn = _layer_norm(alpha * x + mod_ref[2:3, :] * y, lng_ref[0:1, :], lnb_ref[0:1, :])
    xo_ref[...] = xn
    h2 = xn * (1.0 + mod_ref[4:5, :]) + mod_ref[3:4, :]
    h_hi, h_lo = _split_bf16(h2)
    _store_token_tiles(h2_ref, h_hi.astype(F32))
    w_hi, w_lo = _split_bf16(wr_ref[...])
    logits = _nt_dot(w_hi, h_hi) + _nt_dot(w_hi, h_lo) + _nt_dot(w_lo, h_hi) + br_ref[...]

    row8 = lax.broadcasted_iota(jnp.int32, (SUBLANES, tb), 0)
    glog = jnp.where(row8 < N_GROUPS, logits[0:SUBLANES, :], NEG_INF)
    gmax = jnp.max(glog, axis=0, keepdims=True)
    gsel = jnp.min(jnp.where(glog == gmax, row8, SUBLANES), axis=0, keepdims=True)
    gprob = 1.0 / jnp.sum(jnp.exp(glog - gmax), axis=0, keepdims=True)
    elog = jnp.zeros((SUBLANES, tb), F32)
    for g in range(N_GROUPS):
        elog = jnp.where(gsel == g, logits[(g + 1) * SUBLANES:(g + 2) * SUBLANES, :], elog)
    v1 = jnp.max(elog, axis=0, keepdims=True)
    i1 = jnp.min(jnp.where(elog == v1, row8, SUBLANES), axis=0, keepdims=True)
    elog2 = jnp.where(row8 == i1, NEG_INF, elog)
    v2 = jnp.max(elog2, axis=0, keepdims=True)
    i2 = jnp.min(jnp.where(elog2 == v2, row8, SUBLANES), axis=0, keepdims=True)
    t21 = jnp.exp(v2 - v1)
    w1 = gprob / (1.0 + t21)
    w2 = gprob * t21 / (1.0 + t21)
    e1 = gsel * EXPERTS_PER_GROUP + i1
    e2 = gsel * EXPERTS_PER_GROUP + i2

    rowe = lax.broadcasted_iota(jnp.int32, (N_EXPERTS, tb), 0)
    oh1, oh2 = rowe == e1, rowe == e2
    oh = jnp.where(jnp.logical_or(oh1, oh2), 1.0, 0.0)
    before = (lax.broadcasted_iota(jnp.int32, (tb, tb), 0) < lax.broadcasted_iota(jnp.int32, (tb, tb), 1))
    base = _dot(oh.astype(BF16), jnp.where(before, 1.0, 0.0).astype(BF16)) + carry_ref[:, 0:1]
    rank1 = jnp.sum(jnp.where(oh1, base, 0.0), axis=0, keepdims=True).astype(jnp.int32)
    rank2 = jnp.sum(jnp.where(oh2, base, 0.0), axis=0, keepdims=True).astype(jnp.int32)
    carry_ref[...] = carry_ref[...] + jnp.sum(oh, axis=1, keepdims=True)
    cnt_ref[...] = carry_ref[...]

    ri_ref[...] = jnp.where(row8 == 0, e1, jnp.where(row8 == 1, e2, jnp.where(row8 == 2, rank1,
                            jnp.where(row8 == 3, rank2, 0))))
    rowl = lax.broadcasted_iota(jnp.int32, (LANES, tb), 0)
    wcol_ref[...] = jnp.where(rowl == 0, w1, jnp.where(rowl == 1, w2, 0.0)).T


def _oproj_router(o, w_o, x, mod, lng, lnb, w_router, b_router, alpha, first_block):
    split = isinstance(x, tuple)
    assert not (split and first_block)
    bsz, t_in, d = o.shape[0], o.shape[1], w_o.shape[1]
    n_t = t_in // TOKEN_BLOCK - first_block
    t_len = n_t * TOKEN_BLOCK
    n_blk = bsz * n_t
    tok_in = lambda b, t: (b, t + first_block, 0)
    tok = lambda b, t: (b, t, 0)
    blk = lambda b, t: (b * n_t + t, 0, 0)
    const2 = lambda b, t: (0, 0)

    def tok_next(b, t):
        s1 = jnp.minimum(b * n_t + t + 1, n_blk - 1)
        return (s1 // n_t, s1 % n_t + first_block, 0)

    x_specs, x_args = _stream_specs(x, d) if split else ([pl.BlockSpec((None, TOKEN_BLOCK, d), tok_in)], [x])
    return pl.pallas_call(
        functools.partial(_oproj_router_kernel, alpha=alpha, split=split),
        grid=(bsz, n_t),
        in_specs=x_specs + [
            pl.BlockSpec((None, TOKEN_BLOCK, o.shape[-1]), tok_in),
            pl.BlockSpec((None, TOKEN_BLOCK, o.shape[-1]), tok_next),
            pl.BlockSpec(w_o.shape, const2),
            pl.BlockSpec((None, 6, d), lambda b, t: (jnp.where(t + first_block == 0, bsz, b), 0, 0)),
            pl.BlockSpec(lng.shape, const2),
            pl.BlockSpec(lnb.shape, const2),
            pl.BlockSpec(w_router.shape, const2),
            pl.BlockSpec(b_router.shape, const2),
        ],
        out_specs=[
            pl.BlockSpec((None, TOKEN_BLOCK, d), tok),
            pl.BlockSpec((TOKEN_BLOCK * d // LANES, LANES), lambda b, t: (b * n_t + t, 0)),
            pl.BlockSpec((None, SUBLANES, TOKEN_BLOCK), blk),
            pl.BlockSpec((None, TOKEN_BLOCK, LANES), blk),
            pl.BlockSpec((N_EXPERTS, LANES), const2),
        ],
        out_shape=[
            jax.ShapeDtypeStruct((bsz, t_len, d), F32),
            jax.ShapeDtypeStruct((bsz * t_len * d // LANES, LANES), F32),
            jax.ShapeDtypeStruct((n_blk, SUBLANES, TOKEN_BLOCK), jnp.int32),
            jax.ShapeDtypeStruct((n_blk, TOKEN_BLOCK, LANES), F32),
            jax.ShapeDtypeStruct((N_EXPERTS, LANES), F32),
        ],
        scratch_shapes=[pltpu.VMEM((N_EXPERTS, LANES), F32), pltpu.VMEM((2, TOKEN_BLOCK, d), F32)],
        compiler_params=_params(2),
        name="oproj_ln_router",
    )(*x_args, o, o, w_o, mod, lng, lnb, w_router, b_router)


def _for_each_row(n_rows, fn):
    def chunk(c, carry):
        for u in range(ROW_UNROLL):
            fn(c * ROW_UNROLL + u)
        return carry

    lax.fori_loop(0, n_rows // ROW_UNROLL, chunk, 0)


def _expert_row(dest_ref, block, k, r):
    return dest_ref[(block * 2 + k) * TOKEN_BLOCK + r]


def _token_rows(ref, row, n, rpt):
    return ref.at[pl.ds(pl.multiple_of(row * rpt, rpt), n * rpt)]


def _dispatch_kernel(off_ref, tiles_ref, dest_ref, h_ref, xs_ref, zeros, sems, *, rpt):
    tb = h_ref.shape[0] // rpt
    block = pl.program_id(0)

    @pl.when(pl.program_id(0) == 0)
    def _():
        zeros[...] = jnp.zeros_like(zeros)

        def last_tile(e):
            return pltpu.make_async_copy(
                zeros, _token_rows(xs_ref, off_ref[e] + (tiles_ref[e] - 1) * EXPERT_TILE, EXPERT_TILE, rpt), sems.at[2])

        for e in range(N_EXPERTS):
            pl.when(tiles_ref[e] > 0)(lambda e=e: last_tile(e).start())
        for e in range(N_EXPERTS):
            pl.when(tiles_ref[e] > 0)(lambda e=e: last_tile(e).wait())

        def spare_tile(i):
            return pltpu.make_async_copy(zeros, _token_rows(xs_ref, i * EXPERT_TILE, EXPERT_TILE, rpt), sems.at[2])

        first_spare = off_ref[N_EXPERTS - 1] // EXPERT_TILE + tiles_ref[N_EXPERTS - 1]
        n_tiles = xs_ref.shape[0] // (EXPERT_TILE * rpt)
        lax.fori_loop(first_spare, n_tiles, lambda i, c: (spare_tile(i).start(), c)[1], 0)
        lax.fori_loop(first_spare, n_tiles, lambda i, c: (spare_tile(i).wait(), c)[1], 0)

    for sub in range(tb // TOKEN_BLOCK):
        def start(r, sub=sub):
            for k in range(2):
                pltpu.make_async_copy(
                    _token_rows(h_ref, sub * TOKEN_BLOCK + r, 1, rpt),
                    _token_rows(xs_ref, _expert_row(dest_ref, block * (tb // TOKEN_BLOCK) + sub, k, r), 1, rpt),
                    sems.at[k]).start(priority=k)

        _for_each_row(TOKEN_BLOCK, start)
    for k in range(2):
        pltpu.make_async_copy(h_ref, _token_rows(xs_ref, 0, tb, rpt), sems.at[k]).wait()


def _dispatch(row_off, tiles, dest, h2, n_rows, rpt):
    n_blk = h2.shape[0] // (TOKEN_BLOCK * rpt)
    per_step = 2 if n_blk % 2 == 0 else 1
    return pl.pallas_call(
        functools.partial(_dispatch_kernel, rpt=rpt),
        grid_spec=pltpu.PrefetchScalarGridSpec(
            num_scalar_prefetch=3,
            grid=(n_blk // per_step,),
            in_specs=[pl.BlockSpec((per_step * TOKEN_BLOCK * rpt, LANES), lambda i, off, nt, dst: (i, 0))],
            out_specs=pl.BlockSpec(memory_space=pl.ANY),
            scratch_shapes=[pltpu.VMEM((EXPERT_TILE * rpt, LANES), F32), pltpu.SemaphoreType.DMA((3,))],
        ),
        out_shape=jax.ShapeDtypeStruct((n_rows * rpt, LANES), F32),
        compiler_params=_params(1),
        name="moe_dispatch",
    )(row_off, tiles, dest, h2)


def _experts_kernel(te_ref, nu_ref, nxt_ref, xs_ref, wg_hbm, wu_hbm, wd_hbm, ys_ref,
                    wg_f, wu_f, wd_f, wg_s, wu_s, wd_s, slot_ref, sems, *, layer):
    t = pl.program_id(0)

    def fetch(e, slot):
        g, i = e // EXPERTS_PER_GROUP, e % EXPERTS_PER_GROUP
        return [pltpu.make_async_copy(w.at[layer, g, i], buf.at[slot], sems.at[slot, n])
                for n, (w, buf) in enumerate(((wg_hbm, wg_f), (wu_hbm, wu_f), (wd_hbm, wd_f)))]

    @pl.when(t < nu_ref[0])
    def _():
        e = te_ref[t]

        @pl.when(t == 0)
        def _():
            slot_ref[0] = 0
            for c in fetch(e, 0):
                c.start()

        @pl.when(jnp.logical_or(t == 0, e != te_ref[jnp.maximum(t - 1, 0)]))
        def _():
            slot = slot_ref[0]
            for c in fetch(e, slot):
                c.wait()
            nxt = nxt_ref[e]

            @pl.when(nxt >= 0)
            def _():
                for c in fetch(nxt, 1 - slot):
                    c.start()

            wg_s[...] = wg_f[slot].astype(BF16)
            wu_s[...] = wu_f[slot].astype(BF16)
            wd_s[...] = wd_f[slot].astype(BF16)
            slot_ref[0] = 1 - slot

        x = _load_token_tiles(xs_ref, EXPERT_TILE).astype(BF16)
        a = _dot(x, wg_s[...])
        u = _dot(x, wu_s[...])
        hid = a * (1.0 / (1.0 + jnp.exp(-a))) * u
        _store_token_tiles(ys_ref, _dot(hid.astype(BF16), wd_s[...]))

    @pl.when(t >= nu_ref[0])
    def _():
        ys_ref[...] = jnp.zeros_like(ys_ref)


def _experts(tile_expert, n_used, next_expert, xs, w_gate, w_up, w_down, layer):
    d, ff = w_gate.shape[-2:]
    tile_rows = EXPERT_TILE * d // LANES
    n_tiles = xs.shape[0] // tile_rows
    hbm = pl.BlockSpec(memory_space=pl.ANY)
    return pl.pallas_call(
        functools.partial(_experts_kernel, layer=layer),
        grid_spec=pltpu.PrefetchScalarGridSpec(
            num_scalar_prefetch=3,
            grid=(n_tiles,),
            in_specs=[
                pl.BlockSpec((tile_rows, LANES), lambda t, te, nu, nx: (jnp.minimum(t, nu[0] - 1), 0)),
                hbm, hbm, hbm,
            ],
            out_specs=pl.BlockSpec((tile_rows, LANES), lambda t, te, nu, nx: (t, 0)),
            scratch_shapes=[
                pltpu.VMEM((2, d, ff), F32), pltpu.VMEM((2, d, ff), F32), pltpu.VMEM((2, ff, d), F32),
                pltpu.VMEM((d, ff), BF16), pltpu.VMEM((d, ff), BF16), pltpu.VMEM((ff, d), BF16),
                pltpu.SMEM((1,), jnp.int32), pltpu.SemaphoreType.DMA((2, 3)),
            ],
        ),
        out_shape=jax.ShapeDtypeStruct(xs.shape, F32),
        compiler_params=_params(1),
        name="moe_experts",
    )(tile_expert, n_used, next_expert, xs, w_gate, w_up, w_down)


def _combine_kernel(dest_ref, ys_ref, wcol_ref, x_ref, mod_ref, lng_ref, lnb_ref, o_ref, buf, sems, *, alpha):
    tb, d = x_ref.shape
    rpt = d // LANES
    step = pl.program_id(0) * pl.num_programs(1) + pl.program_id(1)
    n_steps = pl.num_programs(0) * pl.num_programs(1)
    slot = step % 2

    def gather(block, into):
        def start(r):
            for k in range(2):
                pltpu.make_async_copy(_token_rows(ys_ref, _expert_row(dest_ref, block, k, r), 1, rpt),
                                      _token_rows(buf.at[into, k], r, 1, rpt), sems.at[into, k]).start(priority=k)

        _for_each_row(tb, start)

    pl.when(step == 0)(lambda: gather(step, 0))
    pl.when(step + 1 < n_steps)(lambda: gather(step + 1, 1 - slot))
    for k in range(2):
        pltpu.make_async_copy(_token_rows(ys_ref, 0, tb, rpt), buf.at[slot, k], sems.at[slot, k]).wait()
    f = (wcol_ref[:, 0:1] * _load_token_tiles(buf.at[slot, 0], tb)
         + wcol_ref[:, 1:2] * _load_token_tiles(buf.at[slot, 1], tb))
    o_ref[...] = _layer_norm(alpha * x_ref[...] + mod_ref[5:6, :] * f, lng_ref[1:2, :], lnb_ref[1:2, :])


def _combine(dest, ys, wcol, x, mod, lng, lnb, alpha, has_ctx):
    bsz, t_len, d = x.shape
    n_t = t_len // TOKEN_BLOCK
    tok = lambda b, t, off: (b, t, 0)
    blk = lambda b, t, off: (b * n_t + t, 0, 0)
    const2 = lambda b, t, off: (0, 0)
    mod_row = (lambda b, t, off: (jnp.where(t == 0, bsz, b), 0, 0)) if has_ctx else (lambda b, t, off: (b, 0, 0))
    return pl.pallas_call(
        functools.partial(_combine_kernel, alpha=alpha),
        grid_spec=pltpu.PrefetchScalarGridSpec(
            num_scalar_prefetch=1,
            grid=(bsz, n_t),
            in_specs=[
                pl.BlockSpec(memory_space=pl.ANY),
                pl.BlockSpec((None, TOKEN_BLOCK, LANES), blk),
                pl.BlockSpec((None, TOKEN_BLOCK, d), tok),
                pl.BlockSpec((None, 6, d), mod_row),
                pl.BlockSpec(lng.shape, const2),
                pl.BlockSpec(lnb.shape, const2),
            ],
            out_specs=pl.BlockSpec((None, TOKEN_BLOCK, d), tok),
            scratch_shapes=[pltpu.VMEM((2, 2, TOKEN_BLOCK * d // LANES, LANES), F32),
                            pltpu.SemaphoreType.DMA((2, 2))],
        ),
        out_shape=jax.ShapeDtypeStruct(x.shape, F32),
        compiler_params=_params(2),
        name="moe_combine_ln",
    )(dest, ys, wcol, x, mod, lng, lnb)


def _qkv_columns(kind):
    if kind == 0:
        perm, *_ = _slot_layout(A_HEAD_DIM // 4)
        nq, nk = A_HEADS, A_KV_HEADS
        q = np.concatenate([s * LANES + perm for s in range(nq)])
        k = np.concatenate([nq * LANES + s * LANES + perm for s in range(nk)])
        v = np.arange((nq + nk) * LANES, (nq + 2 * nk) * LANES)
        return np.concatenate([q, k, v]), nq, nk, nk
    perm, axis_of, freq_of, half_of, _ = _slot_layout(B_HEAD_DIM // 4)
    nf = B_HEAD_DIM // 4
    if kind == 1:
        nq = B_HEADS // 2
        q = np.concatenate([s * LANES + perm for s in range(nq)])
        k_base = B_HEADS * B_HEAD_DIM
        v_base = k_base + B_KV_HEADS * B_HEAD_DIM
        in_head = axis_of * 2 * nf + half_of * nf + freq_of
        k = np.concatenate([k_base + h * B_HEAD_DIM + in_head for h in range(B_KV_HEADS)])
        v = np.concatenate([v_base + h * B_HEAD_DIM + np.arange(LANES) % B_HEAD_DIM for h in range(B_KV_HEADS)])
        return np.concatenate([q, k, v]), nq, B_KV_HEADS, B_KV_HEADS
    nq = C_HEADS
    q = np.concatenate([s * LANES + perm for s in range(nq)])
    k = np.concatenate([nq * LANES + s * LANES + perm for s in range(nq)])
    v = np.arange(2 * nq * LANES, 3 * nq * LANES)
    return np.concatenate([q, k, v]), nq, nq, nq


def kernel(x, c, ctx, c_ctx, w_mod, b_mod, ln_gain, ln_bias, a_w_qkv, a_q_gain, a_k_gain, a_w_o, b_w_qkv, b_sink, b_w_o, c_w_qkv, c_lam_q1, c_lam_k1, c_lam_q2, c_lam_k2, c_subln_gain, c_w_o, moe_w_group, moe_b_group, moe_w_expert, moe_b_expert, moe_w_gate, moe_w_up, moe_w_down):
    bsz, seq, d = x.shape
    n_ctx = ctx.shape[1]
    depth = w_mod.shape[0]
    assert n_ctx == TOKEN_BLOCK and seq % TOKEN_BLOCK == 0 and seq >= TOKEN_BLOCK + 2 * WINDOW
    assert d % (SUBLANES * LANES) == 0
    t_len = n_ctx + seq
    n_tok = bsz * t_len
    alpha = (2.0 * depth) ** 0.25

    mod_rows = -(-(bsz + 1) // SUBLANES) * SUBLANES
    cc = jnp.concatenate([c, c_ctx[None, :], jnp.zeros((mod_rows - bsz - 1, d), F32)], axis=0)
    mod_all = _modulation(cc, w_mod, b_mod).reshape(depth, mod_rows, 6, d)

    stream = (ctx, x) if depth > 1 else jnp.concatenate([ctx, x], axis=1)
    tables = {hd: _rope_tables(seq, n_ctx, hd // 4, hd ** -0.5 * LOG2_E) for hd in (A_HEAD_DIM, B_HEAD_DIM)}
    perm_a = _slot_layout(A_HEAD_DIM // 4)[0]

    for i in range(depth):
        kind, j = i % N_MIXERS, i // N_MIXERS
        mod = mod_all[i]
        cols, n_q, n_k, n_v = _qkv_columns(kind)
        if kind == 0:
            w = a_w_qkv[j][:, cols].astype(BF16)
            gains = (a_q_gain[j][perm_a][None, :], a_k_gain[j][perm_a][None, :])
            q, k, v = _qkv_call(stream, mod, w, tables[A_HEAD_DIM], gains, n_q, n_k, n_v)
            o = _attn_a(q, k, v, n_ctx)
            w_o = a_w_o[j]
        elif kind == 1:
            w = b_w_qkv[j][:, cols].astype(BF16)
            q, k, v = _qkv_call(stream, mod, w, tables[B_HEAD_DIM], None, n_q, n_k, n_v)
            o = _attn_b(b_sink[j], q, k, v, n_ctx)
            w_o = b_w_o[j]
        else:
            w = c_w_qkv[j][:, cols].astype(BF16)
            q, k, v = _qkv_call(stream, mod, w, tables[C_HEAD_DIM], None, n_q, n_k, n_v)
            lam_init = 0.8 - 0.6 * float(np.exp(-0.3 * i))
            lam_vecs = jnp.stack([c_lam_q1[j], c_lam_k1[j], c_lam_q2[j], c_lam_k2[j]], axis=0)
            o = _attn_c(lam_vecs, c_subln_gain[j][None, :], q, k, v, n_ctx, lam_init)
            w_o = c_w_o[j]

        w_router = jnp.zeros((ROUTER_ROWS, d), F32)
        w_router = w_router.at[0:N_GROUPS].set(moe_w_group[i].T).at[SUBLANES:SUBLANES + N_EXPERTS].set(moe_w_expert[i].T)
        b_router = jnp.zeros((ROUTER_ROWS, 1), F32)
        b_router = b_router.at[0:N_GROUPS, 0].set(moe_b_group[i]).at[SUBLANES:SUBLANES + N_EXPERTS, 0].set(moe_b_expert[i])

        last = i == depth - 1
        stream, h2, rinfo, wcol, counts = _oproj_router(
            o, w_o.astype(BF16), stream, mod, ln_gain[i], ln_bias[i], w_router, b_router, alpha, 1 if last else 0)

        n_tiles = -(-2 * rinfo.shape[0] * TOKEN_BLOCK // EXPERT_TILE) + N_EXPERTS
        xs_rows = n_tiles * EXPERT_TILE
        cnt = counts[:, 0].astype(jnp.int32)
        tiles = (cnt + EXPERT_TILE - 1) // EXPERT_TILE
        tile_end = jnp.cumsum(tiles)
        row_off = (tile_end - tiles) * EXPERT_TILE
        n_used = tile_end[-1:].astype(jnp.int32)
        tile_ids = jnp.arange(n_tiles, dtype=jnp.int32)
        tile_expert = jnp.minimum(
            jnp.sum((tile_end[None, :] <= tile_ids[:, None]).astype(jnp.int32), axis=1), N_EXPERTS - 1)

        ids = jnp.arange(N_EXPERTS, dtype=jnp.int32)
        later_used = jnp.logical_and(ids[None, :] > ids[:, None], tiles[None, :] > 0)
        next_expert = jnp.min(jnp.where(later_used, ids[None, :], N_EXPERTS), axis=1)
        next_expert = jnp.where(next_expert == N_EXPERTS, -1, next_expert).astype(jnp.int32)

        expert_start = jnp.sum(jnp.where(rinfo[:, 0:2, :, None] == ids, row_off, 0), axis=-1)
        dest = (expert_start + rinfo[:, 2:4, :]).reshape(-1)

        xs = _dispatch(row_off, tiles, dest, h2, xs_rows, d // LANES)
        ys = _experts(tile_expert, n_used, next_expert, xs, moe_w_gate, moe_w_up, moe_w_down, i)
        stream = _combine(dest, ys, wcol, stream, mod, ln_gain[i], ln_bias[i], alpha, not last)

    return stream
```

```python
import functools

import numpy as np
import jax
import jax.numpy as jnp
from jax import lax
from jax.experimental import pallas as pl
from jax.experimental.pallas import tpu as pltpu

GRID_W = 64
ROPE_THETA = 10000.0
LN_EPS = 1e-5
RMS_EPS = 1e-6
NEG_INF = -1e30
N_MIXERS = 3
LOG2_E = float(np.log2(np.e))

A_HEADS, A_KV_HEADS, A_HEAD_DIM = 8, 2, 128
B_HEADS, B_KV_HEADS, B_HEAD_DIM = 16, 4, 64
WINDOW = 128
C_HEADS, C_HEAD_DIM = 8, 64
N_GROUPS, EXPERTS_PER_GROUP = 4, 8
N_EXPERTS = N_GROUPS * EXPERTS_PER_GROUP

LANES = 128
SUBLANES = 8
HALF = LANES // 2
TOKEN_BLOCK = 256
EXPERT_TILE = 512
ROUTER_ROWS = 128
ROW_UNROLL = 8
C_HEADS_PER_STEP = 4
VMEM_LIMIT = 56 * 1024 * 1024

F32 = jnp.float32
BF16 = jnp.bfloat16


def _nt_dot(a, b):
    return lax.dot_general(a, b, (((1,), (1,)), ((), ())), preferred_element_type=F32)


def _dot(a, b):
    return jnp.dot(a, b, preferred_element_type=F32)


def _split_bf16(x):
    hi = x.astype(BF16)
    lo = (x - hi.astype(F32)).astype(BF16)
    return hi, lo


def _params(n_axes):
    return pltpu.CompilerParams(dimension_semantics=("arbitrary",) * n_axes, vmem_limit_bytes=VMEM_LIMIT)


def _slot_layout(n_freq):
    n_sub = LANES // (4 * n_freq)
    perm = np.zeros(LANES, np.int32)
    axis_of = np.zeros(LANES, np.int32)
    freq_of = np.zeros(LANES, np.int32)
    half_of = np.zeros(LANES, np.int32)
    sub_of = np.zeros(LANES, np.int32)
    for half in range(2):
        for sub in range(n_sub):
            for axis in range(2):
                for f in range(n_freq):
                    l = half * HALF + sub * 2 * n_freq + axis * n_freq + f
                    perm[l] = sub * 4 * n_freq + axis * 2 * n_freq + half * n_freq + f
                    axis_of[l], freq_of[l], half_of[l], sub_of[l] = axis, f, half, sub
    return perm, axis_of, freq_of, half_of, sub_of


def _rope_tables(seq, ctx, n_freq, q_scale):
    _, axis_of, freq_of, half_of, _ = _slot_layout(n_freq)
    rows = seq // GRID_W
    row = jnp.broadcast_to(jnp.arange(rows, dtype=F32)[:, None], (rows, GRID_W)).reshape(-1)
    col = jnp.broadcast_to(jnp.arange(GRID_W, dtype=F32)[None, :], (rows, GRID_W)).reshape(-1)
    inv = ROPE_THETA ** (-jnp.arange(n_freq, dtype=F32) / n_freq)
    ang = jnp.stack([row[:, None] * inv, col[:, None] * inv], axis=1)
    cos, sin = jnp.cos(ang), jnp.sin(ang)
    cos_l = cos[:, axis_of, freq_of]
    sin_l = sin[:, axis_of, freq_of] * jnp.asarray(np.where(half_of == 0, -1.0, 1.0), F32)
    cos_l = jnp.concatenate([jnp.ones((ctx, LANES), F32), cos_l], axis=0)
    sin_l = jnp.concatenate([jnp.zeros((ctx, LANES), F32), sin_l], axis=0)
    return cos_l * q_scale, sin_l * q_scale, cos_l, sin_l


def _mod_kernel(c_ref, w_ref, b_ref, o_ref):
    c = c_ref[...]
    s = c * (1.0 / (1.0 + jnp.exp(-c)))
    s_hi, s_lo = _split_bf16(s)
    w_hi, w_lo = _split_bf16(w_ref[...])
    o_ref[...] = _dot(s_hi, w_hi) + _dot(s_lo, w_hi) + _dot(s_hi, w_lo) + b_ref[...]


def _modulation(cc, w_mod, b_mod):
    depth, d, n = w_mod.shape
    rows = cc.shape[0]
    tn = n // 4
    return pl.pallas_call(
        _mod_kernel,
        grid=(depth, n // tn),
        in_specs=[
            pl.BlockSpec((rows, d), lambda l, j: (0, 0)),
            pl.BlockSpec((None, d, tn), lambda l, j: (l, 0, j)),
            pl.BlockSpec((None, 1, tn), lambda l, j: (l, 0, j)),
        ],
        out_specs=pl.BlockSpec((None, rows, tn), lambda l, j: (l, 0, j)),
        out_shape=jax.ShapeDtypeStruct((depth, rows, n), F32),
        compiler_params=_params(2),
        name="modulation",
    )(cc, w_mod, b_mod.reshape(depth, 1, n))


def _stream_block(refs, split):
    if not split:
        return refs.pop(0)[...]
    ctx_ref, lat_ref = refs.pop(0), refs.pop(0)
    return jnp.where(pl.program_id(1) == 0, ctx_ref[...], lat_ref[...])


def _stream_specs(x, d):
    if not isinstance(x, tuple):
        return [pl.BlockSpec((None, TOKEN_BLOCK, d), lambda b, t, *_: (b, t, 0))], [x]
    return ([pl.BlockSpec((None, TOKEN_BLOCK, d), lambda b, t, *_: (b, 0, 0)),
             pl.BlockSpec((None, TOKEN_BLOCK, d), lambda b, t, *_: (b, jnp.maximum(t - 1, 0), 0))], list(x))


def _qkv_kernel(*refs, n_q, n_k, rms, split):
    refs = list(refs)
    x = _stream_block(refs, split)
    if rms:
        mod_ref, w_ref, cq_ref, sq_ref, ck_ref, sk_ref, gq_ref, gk_ref, q_ref, k_ref, v_ref = refs
    else:
        mod_ref, w_ref, cq_ref, sq_ref, ck_ref, sk_ref, q_ref, k_ref, v_ref = refs
        gq_ref = gk_ref = None
    sh, sc = mod_ref[0:1, :], mod_ref[1:2, :]
    h = (x * (1.0 + sc) + sh).astype(BF16)

    def rope_slots(y, cos, sin, gain_ref, out_ref, n_slots):
        for j in range(n_slots):
            yj = y[:, j * LANES:(j + 1) * LANES]
            if gain_ref is not None:
                ms = jnp.mean(yj * yj, axis=-1, keepdims=True)
                yj = yj * lax.rsqrt(ms + RMS_EPS) * gain_ref[...]
            out = yj * cos + pltpu.roll(yj, HALF, 1) * sin
            out_ref[:, j * LANES:(j + 1) * LANES] = out.astype(BF16)

    nq, nk = n_q * LANES, n_k * LANES
    rope_slots(_dot(h, w_ref[:, 0:nq]), cq_ref[...], sq_ref[...], gq_ref, q_ref, n_q)
    rope_slots(_dot(h, w_ref[:, nq:nq + nk]), ck_ref[...], sk_ref[...], gk_ref, k_ref, n_k)
    v_ref[...] = _dot(h, w_ref[:, nq + nk:]).astype(BF16)


def _qkv_call(x, mod, w, tables, gains, n_q, n_k, n_v):
    split = isinstance(x, tuple)
    bsz, d = (x[0] if split else x).shape[0::2]
    t_len = x[0].shape[1] + x[1].shape[1] if split else x.shape[1]
    n_t = t_len // TOKEN_BLOCK
    rms = gains is not None
    tok = lambda b, t: (b, t, 0)
    tab_spec = pl.BlockSpec((TOKEN_BLOCK, LANES), lambda b, t: (t, 0))
    x_specs, x_args = _stream_specs(x, d)
    in_specs = x_specs + [
        pl.BlockSpec((None, 6, d), lambda b, t: (jnp.where(t == 0, bsz, b), 0, 0)),
        pl.BlockSpec(w.shape, lambda b, t: (0, 0)),
        tab_spec, tab_spec, tab_spec, tab_spec,
    ]
    args = x_args + [mod, w, *tables]
    if rms:
        in_specs += [pl.BlockSpec((1, LANES), lambda b, t: (0, 0))] * 2
        args += list(gains)
    return pl.pallas_call(
        functools.partial(_qkv_kernel, n_q=n_q, n_k=n_k, rms=rms, split=split),
        grid=(bsz, n_t),
        in_specs=in_specs,
        out_specs=[pl.BlockSpec((None, TOKEN_BLOCK, n * LANES), tok) for n in (n_q, n_k, n_v)],
        out_shape=[jax.ShapeDtypeStruct((bsz, t_len, n * LANES), BF16) for n in (n_q, n_k, n_v)],
        compiler_params=_params(2),
        name="qkv_rope",
    )(*args)


def _softmax_pv(s, v_ext):
    e = jnp.exp2(s - jnp.max(s, axis=-1, keepdims=True))
    oe = _dot(e.astype(BF16), v_ext)
    return oe[:, :LANES], oe[:, LANES:]


def _fill_v_ext(v_ref, v_ext):
    v_ext[:, :LANES] = v_ref[...]
    v_ext[:, LANES:] = jnp.ones((v_ref.shape[0], LANES), BF16)


def _attn_a_kernel(q_ref, k_ref, v_ref, o_ref, v_ext, *, ctx, group):
    t = pl.program_id(2)
    pl.when(t == 0)(lambda: _fill_v_ext(v_ref, v_ext))

    def attend(n_keys):
        k, v = k_ref[0:n_keys, :], v_ext[0:n_keys, :]
        for g in range(group):
            num, den = _softmax_pv(_nt_dot(q_ref[:, g * LANES:(g + 1) * LANES], k), v)
            o_ref[:, g * LANES:(g + 1) * LANES] = (num / den).astype(BF16)

    pl.when(t == 0)(lambda: attend(ctx))
    pl.when(t > 0)(lambda: attend(k_ref.shape[0]))


def _attn_a(q, k, v, ctx):
    bsz, t_len, _ = q.shape
    group = A_HEADS // A_KV_HEADS
    qo_spec = pl.BlockSpec((None, TOKEN_BLOCK, group * LANES), lambda b, h, t: (b, t, h))
    kv_spec = pl.BlockSpec((None, t_len, LANES), lambda b, h, t: (b, 0, h))
    return pl.pallas_call(
        functools.partial(_attn_a_kernel, ctx=ctx, group=group),
        grid=(bsz, A_KV_HEADS, t_len // TOKEN_BLOCK),
        in_specs=[qo_spec, kv_spec, kv_spec],
        out_specs=qo_spec,
        out_shape=jax.ShapeDtypeStruct(q.shape, BF16),
        scratch_shapes=[pltpu.VMEM((t_len, 2 * LANES), BF16)],
        compiler_params=_params(3),
        name="attn_dense_gqa",
    )(q, k, v)


def _attn_b_kernel(sink_ref, q_ref, k_ref, v_ref, o_ref, v_ext, *, ctx, seq, group):
    t = pl.program_id(1)
    lane = lax.broadcasted_iota(jnp.int32, (1, LANES), 1)
    first_head = (lane // (HALF // 2)) % 2 == 0
    lower = lane < HALF
    span = TOKEN_BLOCK + 2 * WINDOW
    slots_per_kv = group // 2

    @pl.when(t == 0)
    def _():
        lower_all = lax.broadcasted_iota(jnp.int32, (1, v_ref.shape[1]), 1) % LANES < HALF
        v_ext[...] = jnp.where(lower_all, v_ref[...], jnp.ones_like(v_ref))

    def run(keys, values, valid):
        for h in range(B_KV_HEADS):
            kk, vv = keys(h), values(h)
            for sl in range(h * slots_per_kv, (h + 1) * slots_per_kv):
                qs = q_ref[:, sl * LANES:(sl + 1) * LANES]
                outs = []
                for j in range(2):
                    qm = jnp.where(first_head if j == 0 else jnp.logical_not(first_head), qs, jnp.zeros_like(qs))
                    s = _nt_dot(qm, kk)
                    if valid is not None:
                        s = jnp.concatenate([s[:, :ctx], jnp.where(valid, s[:, ctx:], NEG_INF)], axis=1)
                    snk = sink_ref[sl * 2 + j] * LOG2_E
                    m = jnp.maximum(jnp.max(s, axis=-1, keepdims=True), snk)
                    oe = _dot(jnp.exp2(s - m).astype(BF16), vv)
                    outs.append((oe, pltpu.roll(oe, HALF, 1), jnp.exp2(snk - m)))
                (oa, ra, za), (ob, rb, zb) = outs
                num = jnp.where(lower, oa, rb)
                den = jnp.where(lower, ra + za, ob + zb)
                o_ref[:, sl * LANES:(sl + 1) * LANES] = (num / den).astype(BF16)

    def head(ref, h, rows):
        return ref[rows, h * LANES:(h + 1) * LANES]

    @pl.when(t == 0)
    def _():
        rows = slice(0, ctx)
        run(lambda h: head(k_ref, h, rows), lambda h: head(v_ext, h, rows), None)

    @pl.when(t > 0)
    def _():
        p0 = (t - 1) * TOKEN_BLOCK
        start = pl.multiple_of(jnp.clip(p0 - WINDOW, 0, seq - span), WINDOW)

        def ctx_and_window(ref, h):
            return jnp.concatenate([head(ref, h, slice(0, ctx)), head(ref, h, pl.ds(ctx + start, span))], axis=0)

        dist = (lax.broadcasted_iota(jnp.int32, (TOKEN_BLOCK, span), 1)
                - lax.broadcasted_iota(jnp.int32, (TOKEN_BLOCK, span), 0) + (start - p0))
        run(lambda h: ctx_and_window(k_ref, h), lambda h: ctx_and_window(v_ext, h), jnp.abs(dist) <= WINDOW)


def _attn_b(sink, q, k, v, ctx):
    bsz, t_len, _ = q.shape
    group = B_HEADS // B_KV_HEADS
    qo_spec = pl.BlockSpec((None, TOKEN_BLOCK, q.shape[-1]), lambda b, t, s: (b, t, 0))
    kv_spec = pl.BlockSpec((None, t_len, k.shape[-1]), lambda b, t, s: (b, 0, 0))
    return pl.pallas_call(
        functools.partial(_attn_b_kernel, ctx=ctx, seq=t_len - ctx, group=group),
        grid_spec=pltpu.PrefetchScalarGridSpec(
            num_scalar_prefetch=1,
            grid=(bsz, t_len // TOKEN_BLOCK),
            in_specs=[qo_spec, kv_spec, kv_spec],
            out_specs=qo_spec,
            scratch_shapes=[pltpu.VMEM((t_len, k.shape[-1]), BF16)],
        ),
        out_shape=jax.ShapeDtypeStruct(q.shape, BF16),
        compiler_params=_params(2),
        name="attn_window_sink",
    )(sink, q, k, v)


def _attn_c_kernel(lam_ref, gain_ref, q_ref, k_ref, v_ref, o_ref, *, ctx, lam_init):
    t = pl.program_id(2)
    lane = lax.broadcasted_iota(jnp.int32, (1, LANES), 1)
    first_map = (lane // (HALF // 2)) % 2 == 0
    lam = (jnp.exp(jnp.sum(lam_ref[0:1, :] * lam_ref[1:2, :], axis=-1, keepdims=True))
           - jnp.exp(jnp.sum(lam_ref[2:3, :] * lam_ref[3:4, :], axis=-1, keepdims=True)) + lam_init)

    def attend(n_keys):
        for h in range(q_ref.shape[1] // LANES):
            head = slice(h * LANES, (h + 1) * LANES)
            k, v = k_ref[0:n_keys, head], v_ref[0:n_keys, head]
            q = q_ref[:, head]
            zero = jnp.zeros_like(q)
            s0 = _nt_dot(jnp.where(first_map, q, zero), k)
            s1 = _nt_dot(jnp.where(first_map, zero, q), k)
            e0 = jnp.exp2(s0 - jnp.max(s0, axis=-1, keepdims=True))
            e1 = jnp.exp2(s1 - jnp.max(s1, axis=-1, keepdims=True))
            r0 = 1.0 / jnp.sum(e0, axis=-1, keepdims=True)
            r1 = lam / jnp.sum(e1, axis=-1, keepdims=True)
            o = _dot((e0 * r0 - e1 * r1).astype(BF16), v)
            y = o * lax.rsqrt(jnp.mean(o * o, axis=-1, keepdims=True) + RMS_EPS) * gain_ref[...]
            o_ref[:, head] = (y * (1.0 - lam_init)).astype(BF16)

    pl.when(t == 0)(lambda: attend(ctx))
    pl.when(t > 0)(lambda: attend(k_ref.shape[0]))


def _attn_c(lam_vecs, gain, q, k, v, ctx, lam_init):
    bsz, t_len, _ = q.shape
    width = C_HEADS_PER_STEP * LANES
    qo_spec = pl.BlockSpec((None, TOKEN_BLOCK, width), lambda b, h, t: (b, t, h))
    kv_spec = pl.BlockSpec((None, t_len, width), lambda b, h, t: (b, 0, h))
    return pl.pallas_call(
        functools.partial(_attn_c_kernel, ctx=ctx, lam_init=lam_init),
        grid=(bsz, C_HEADS // C_HEADS_PER_STEP, t_len // TOKEN_BLOCK),
        in_specs=[pl.BlockSpec(lam_vecs.shape, lambda b, h, t: (0, 0)),
                  pl.BlockSpec(gain.shape, lambda b, h, t: (0, 0)),
                  qo_spec, kv_spec, kv_spec],
        out_specs=qo_spec,
        out_shape=jax.ShapeDtypeStruct(q.shape, BF16),
        compiler_params=_params(3),
        name="attn_differential",
    )(lam_vecs, gain, q, k, v)


def _store_token_tiles(ref, x):
    rows, d = x.shape
    rpt = d // LANES
    for c in range(rpt):
        ref[pl.ds(c, rows, stride=rpt), :] = x[:, c * LANES:(c + 1) * LANES]


def _load_token_tiles(ref, rows):
    rpt = ref.shape[0] // rows
    return jnp.concatenate([ref[pl.ds(c, rows, stride=rpt), :] for c in range(rpt)], axis=1)


def _layer_norm(z, gain, bias):
    mu = jnp.mean(z, axis=-1, keepdims=True)
    zc = z - mu
    var = jnp.mean(zc * zc, axis=-1, keepdims=True)
    return zc * lax.rsqrt(var + LN_EPS) * gain + bias


def _oproj_router_kernel(*refs, alpha, split):
    refs = list(refs)
    x = _stream_block(refs, split)
    (o_ref, o_next_ref, wo_ref, mod_ref, lng_ref, lnb_ref, wr_ref, br_ref,
     xo_ref, h2_ref, ri_ref, wcol_ref, cnt_ref, carry_ref, y_ref) = refs
    step = pl.program_id(0) * pl.num_programs(1) + pl.program_id(1)
    slot = step % 2

    @pl.when(step == 0)
    def _():
        carry_ref[...] = jnp.zeros_like(carry_ref)
        y_ref[0] = _dot(o_ref[...], wo_ref[...])

    tb = x.shape[0]
    y = y_ref[slot]
    y_ref[1 - slot] = _dot(o_next_ref[...], wo_ref[...])
    xn = _layer_norm(alpha * x + mod_ref[2:3, :] * y, lng_ref[0:1, :], lnb_ref[0:1, :])
    xo_ref[...] = xn
    h2 = xn * (1.0 + mod_ref[4:5, :]) + mod_ref[3:4, :]
    h_hi, h_lo = _split_bf16(h2)
    _store_token_tiles(h2_ref, h_hi.astype(F32))
    w_hi, w_lo = _split_bf16(wr_ref[...])
    logits = _nt_dot(w_hi, h_hi) + _nt_dot(w_hi, h_lo) + _nt_dot(w_lo, h_hi) + br_ref[...]

    row8 = lax.broadcasted_iota(jnp.int32, (SUBLANES, tb), 0)
    glog = jnp.where(row8 < N_GROUPS, logits[0:SUBLANES, :], NEG_INF)
    gmax = jnp.max(glog, axis=0, keepdims=True)
    gsel = jnp.min(jnp.where(glog == gmax, row8, SUBLANES), axis=0, keepdims=True)
    gprob = 1.0 / jnp.sum(jnp.exp(glog - gmax), axis=0, keepdims=True)
    elog = jnp.zeros((SUBLANES, tb), F32)
    for g in range(N_GROUPS):
        elog = jnp.where(gsel == g, logits[(g + 1) * SUBLANES:(g + 2) * SUBLANES, :], elog)
    v1 = jnp.max(elog, axis=0, keepdims=True)
    i1 = jnp.min(jnp.where(elog == v1, row8, SUBLANES), axis=0, keepdims=True)
    elog2 = jnp.where(row8 == i1, NEG_INF, elog)
    v2 = jnp.max(elog2, axis=0, keepdims=True)
    i2 = jnp.min(jnp.where(elog2 == v2, row8, SUBLANES), axis=0, keepdims=True)
    t21 = jnp.exp(v2 - v1)
    w1 = gprob / (1.0 + t21)
    w2 = gprob * t21 / (1.0 + t21)
    e1 = gsel * EXPERTS_PER_GROUP + i1
    e2 = gsel * EXPERTS_PER_GROUP + i2

    rowe = lax.broadcasted_iota(jnp.int32, (N_EXPERTS, tb), 0)
    oh1, oh2 = rowe == e1, rowe == e2
    oh = jnp.where(jnp.logical_or(oh1, oh2), 1.0, 0.0)
    before = (lax.broadcasted_iota(jnp.int32, (tb, tb), 0) < lax.broadcasted_iota(jnp.int32, (tb, tb), 1))
    base = _dot(oh.astype(BF16), jnp.where(before, 1.0, 0.0).astype(BF16)) + carry_ref[:, 0:1]
    rank1 = jnp.sum(jnp.where(oh1, base, 0.0), axis=0, keepdims=True).astype(jnp.int32)
    rank2 = jnp.sum(jnp.where(oh2, base, 0.0), axis=0, keepdims=True).astype(jnp.int32)
    carry_ref[...] = carry_ref[...] + jnp.sum(oh, axis=1, keepdims=True)
    cnt_ref[...] = carry_ref[...]

    ri_ref[...] = jnp.where(row8 == 0, e1, jnp.where(row8 == 1, e2, jnp.where(row8 == 2, rank1,
                            jnp.where(row8 == 3, rank2, 0))))
    rowl = lax.broadcasted_iota(jnp.int32, (LANES, tb), 0)
    wcol_ref[...] = jnp.where(rowl == 0, w1, jnp.where(rowl == 1, w2, 0.0)).T


def _oproj_router(o, w_o, x, mod, lng, lnb, w_router, b_router, alpha, first_block):
    split = isinstance(x, tuple)
    assert not (split and first_block)
    bsz, t_in, d = o.shape[0], o.shape[1], w_o.shape[1]
    n_t = t_in // TOKEN_BLOCK - first_block
    t_len = n_t * TOKEN_BLOCK
    n_blk = bsz * n_t
    tok_in = lambda b, t: (b, t + first_block, 0)
    tok = lambda b, t: (b, t, 0)
    blk = lambda b, t: (b * n_t + t, 0, 0)
    const2 = lambda b, t: (0, 0)

    def tok_next(b, t):
        s1 = jnp.minimum(b * n_t + t + 1, n_blk - 1)
        return (s1 // n_t, s1 % n_t + first_block, 0)

    x_specs, x_args = _stream_specs(x, d) if split else ([pl.BlockSpec((None, TOKEN_BLOCK, d), tok_in)], [x])
    return pl.pallas_call(
        functools.partial(_oproj_router_kernel, alpha=alpha, split=split),
        grid=(bsz, n_t),
        in_specs=x_specs + [
            pl.BlockSpec((None, TOKEN_BLOCK, o.shape[-1]), tok_in),
            pl.BlockSpec((None, TOKEN_BLOCK, o.shape[-1]), tok_next),
            pl.BlockSpec(w_o.shape, const2),
            pl.BlockSpec((None, 6, d), lambda b, t: (jnp.where(t + first_block == 0, bsz, b), 0, 0)),
            pl.BlockSpec(lng.shape, const2),
            pl.BlockSpec(lnb.shape, const2),
            pl.BlockSpec(w_router.shape, const2),
            pl.BlockSpec(b_router.shape, const2),
        ],
        out_specs=[
            pl.BlockSpec((None, TOKEN_BLOCK, d), tok),
            pl.BlockSpec((TOKEN_BLOCK * d // LANES, LANES), lambda b, t: (b * n_t + t, 0)),
            pl.BlockSpec((None, SUBLANES, TOKEN_BLOCK), blk),
            pl.BlockSpec((None, TOKEN_BLOCK, LANES), blk),
            pl.BlockSpec((N_EXPERTS, LANES), const2),
        ],
        out_shape=[
            jax.ShapeDtypeStruct((bsz, t_len, d), F32),
            jax.ShapeDtypeStruct((bsz * t_len * d // LANES, LANES), F32),
            jax.ShapeDtypeStruct((n_blk, SUBLANES, TOKEN_BLOCK), jnp.int32),
            jax.ShapeDtypeStruct((n_blk, TOKEN_BLOCK, LANES), F32),
            jax.ShapeDtypeStruct((N_EXPERTS, LANES), F32),
        ],
        scratch_shapes=[pltpu.VMEM((N_EXPERTS, LANES), F32), pltpu.VMEM((2, TOKEN_BLOCK, d), F32)],
        compiler_params=_params(2),
        name="oproj_ln_router",
    )(*x_args, o, o, w_o, mod, lng, lnb, w_router, b_router)


def _for_each_row(n_rows, fn):
    def chunk(c, carry):
        for u in range(ROW_UNROLL):
            fn(c * ROW_UNROLL + u)
        return carry

    lax.fori_loop(0, n_rows // ROW_UNROLL, chunk, 0)


def _expert_row(dest_ref, block, k, r):
    return dest_ref[(block * 2 + k) * TOKEN_BLOCK + r]


def _token_rows(ref, row, n, rpt):
    return ref.at[pl.ds(pl.multiple_of(row * rpt, rpt), n * rpt)]


def _dispatch_kernel(off_ref, tiles_ref, dest_ref, h_ref, h_hbm, xs_ref, zeros, sems, *, rpt):
    tb = h_ref.shape[0] // rpt
    block = pl.program_id(0)

    @pl.when(pl.program_id(0) == 0)
    def _():
        zeros[...] = jnp.zeros_like(zeros)

        def last_tile(e):
            return pltpu.make_async_copy(
                zeros, _token_rows(xs_ref, off_ref[e] + (tiles_ref[e] - 1) * EXPERT_TILE, EXPERT_TILE, rpt), sems.at[2])

        for e in range(N_EXPERTS):
            pl.when(tiles_ref[e] > 0)(lambda e=e: last_tile(e).start())
        for e in range(N_EXPERTS):
            pl.when(tiles_ref[e] > 0)(lambda e=e: last_tile(e).wait())

        def spare_tile(i):
            return pltpu.make_async_copy(zeros, _token_rows(xs_ref, i * EXPERT_TILE, EXPERT_TILE, rpt), sems.at[2])

        first_spare = off_ref[N_EXPERTS - 1] // EXPERT_TILE + tiles_ref[N_EXPERTS - 1]
        n_tiles = xs_ref.shape[0] // (EXPERT_TILE * rpt)
        lax.fori_loop(first_spare, n_tiles, lambda i, c: (spare_tile(i).start(), c)[1], 0)
        lax.fori_loop(first_spare, n_tiles, lambda i, c: (spare_tile(i).wait(), c)[1], 0)

    for sub in range(tb // TOKEN_BLOCK):
        def start(r, sub=sub):
            row = sub * TOKEN_BLOCK + r
            for k, src in ((0, _token_rows(h_ref, row, 1, rpt)), (1, _token_rows(h_hbm, block * tb + row, 1, rpt))):
                pltpu.make_async_copy(
                    src, _token_rows(xs_ref, _expert_row(dest_ref, block * (tb // TOKEN_BLOCK) + sub, k, r), 1, rpt),
                    sems.at[k]).start(priority=k)

        _for_each_row(TOKEN_BLOCK, start)
    pltpu.make_async_copy(h_ref, _token_rows(xs_ref, 0, tb, rpt), sems.at[0]).wait()
    pltpu.make_async_copy(_token_rows(h_hbm, 0, tb, rpt), _token_rows(xs_ref, 0, tb, rpt), sems.at[1]).wait()


def _dispatch(row_off, tiles, dest, h2, n_rows, rpt):
    n_blk = h2.shape[0] // (TOKEN_BLOCK * rpt)
    per_step = 2 if n_blk % 2 == 0 else 1
    return pl.pallas_call(
        functools.partial(_dispatch_kernel, rpt=rpt),
        grid_spec=pltpu.PrefetchScalarGridSpec(
            num_scalar_prefetch=3,
            grid=(n_blk // per_step,),
            in_specs=[pl.BlockSpec((per_step * TOKEN_BLOCK * rpt, LANES), lambda i, off, nt, dst: (i, 0)),
                      pl.BlockSpec(memory_space=pl.ANY)],
            out_specs=pl.BlockSpec(memory_space=pl.ANY),
            scratch_shapes=[pltpu.VMEM((EXPERT_TILE * rpt, LANES), F32), pltpu.SemaphoreType.DMA((3,))],
        ),
        out_shape=jax.ShapeDtypeStruct((n_rows * rpt, LANES), F32),
        compiler_params=_params(1),
        name="moe_dispatch",
    )(row_off, tiles, dest, h2, h2)


def _experts_kernel(te_ref, nu_ref, nxt_ref, xs_ref, wg_hbm, wu_hbm, wd_hbm, ys_ref,
                    wg_f, wu_f, wd_f, wg_s, wu_s, wd_s, slot_ref, sems, *, layer):
    t = pl.program_id(0)

    def fetch(e, slot):
        g, i = e // EXPERTS_PER_GROUP, e % EXPERTS_PER_GROUP
        return [pltpu.make_async_copy(w.at[layer, g, i], buf.at[slot], sems.at[slot, n])
                for n, (w, buf) in enumerate(((wg_hbm, wg_f), (wu_hbm, wu_f), (wd_hbm, wd_f)))]

    @pl.when(t < nu_ref[0])
    def _():
        e = te_ref[t]

        @pl.when(t == 0)
        def _():
            slot_ref[0] = 0
            for c in fetch(e, 0):
                c.start()

        @pl.when(jnp.logical_or(t == 0, e != te_ref[jnp.maximum(t - 1, 0)]))
        def _():
            slot = slot_ref[0]
            for c in fetch(e, slot):
                c.wait()
            nxt = nxt_ref[e]

            @pl.when(nxt >= 0)
            def _():
                for c in fetch(nxt, 1 - slot):
                    c.start()

            wg_s[...] = wg_f[slot].astype(BF16)
            wu_s[...] = wu_f[slot].astype(BF16)
            wd_s[...] = wd_f[slot].astype(BF16)
            slot_ref[0] = 1 - slot

        x = _load_token_tiles(xs_ref, EXPERT_TILE).astype(BF16)
        a = _dot(x, wg_s[...])
        u = _dot(x, wu_s[...])
        hid = a * (1.0 / (1.0 + jnp.exp(-a))) * u
        _store_token_tiles(ys_ref, _dot(hid.astype(BF16), wd_s[...]))

    @pl.when(t >= nu_ref[0])
    def _():
        ys_ref[...] = jnp.zeros_like(ys_ref)


def _experts(tile_expert, n_used, next_expert, xs, w_gate, w_up, w_down, layer):
    d, ff = w_gate.shape[-2:]
    tile_rows = EXPERT_TILE * d // LANES
    n_tiles = xs.shape[0] // tile_rows
    hbm = pl.BlockSpec(memory_space=pl.ANY)
    return pl.pallas_call(
        functools.partial(_experts_kernel, layer=layer),
        grid_spec=pltpu.PrefetchScalarGridSpec(
            num_scalar_prefetch=3,
            grid=(n_tiles,),
            in_specs=[
                pl.BlockSpec((tile_rows, LANES), lambda t, te, nu, nx: (jnp.minimum(t, nu[0] - 1), 0)),
                hbm, hbm, hbm,
            ],
            out_specs=pl.BlockSpec((tile_rows, LANES), lambda t, te, nu, nx: (t, 0)),
            scratch_shapes=[
                pltpu.VMEM((2, d, ff), F32), pltpu.VMEM((2, d, ff), F32), pltpu.VMEM((2, ff, d), F32),
                pltpu.VMEM((d, ff), BF16), pltpu.VMEM((d, ff), BF16), pltpu.VMEM((ff, d), BF16),
                pltpu.SMEM((1,), jnp.int32), pltpu.SemaphoreType.DMA((2, 3)),
            ],
        ),
        out_shape=jax.ShapeDtypeStruct(xs.shape, F32),
        compiler_params=_params(1),
        name="moe_experts",
    )(tile_expert, n_used, next_expert, xs, w_gate, w_up, w_down)


def _combine_kernel(dest_ref, ys_ref, wcol_ref, x_ref, mod_ref, lng_ref, lnb_ref, o_ref, buf, sems, *, alpha):
    tb, d = x_ref.shape
    rpt = d // LANES
    step = pl.program_id(0) * pl.num_programs(1) + pl.program_id(1)
    n_steps = pl.num_programs(0) * pl.num_programs(1)
    slot = step % 2

    def gather(block, into):
        def start(r):
            for k in range(2):
                pltpu.make_async_copy(_token_rows(ys_ref, _expert_row(dest_ref, block, k, r), 1, rpt),
                                      _token_rows(buf.at[into, k], r, 1, rpt), sems.at[into, k]).start(priority=k)

        _for_each_row(tb, start)

    pl.when(step == 0)(lambda: gather(step, 0))
    pl.when(step + 1 < n_steps)(lambda: gather(step + 1, 1 - slot))
    for k in range(2):
        pltpu.make_async_copy(_token_rows(ys_ref, 0, tb, rpt), buf.at[slot, k], sems.at[slot, k]).wait()
    f = (wcol_ref[:, 0:1] * _load_token_tiles(buf.at[slot, 0], tb)
         + wcol_ref[:, 1:2] * _load_token_tiles(buf.at[slot, 1], tb))
    o_ref[...] = _layer_norm(alpha * x_ref[...] + mod_ref[5:6, :] * f, lng_ref[1:2, :], lnb_ref[1:2, :])


def _combine(dest, ys, wcol, x, mod, lng, lnb, alpha, has_ctx):
    bsz, t_len, d = x.shape
    n_t = t_len // TOKEN_BLOCK
    tok = lambda b, t, off: (b, t, 0)
    blk = lambda b, t, off: (b * n_t + t, 0, 0)
    const2 = lambda b, t, off: (0, 0)
    mod_row = (lambda b, t, off: (jnp.where(t == 0, bsz, b), 0, 0)) if has_ctx else (lambda b, t, off: (b, 0, 0))
    return pl.pallas_call(
        functools.partial(_combine_kernel, alpha=alpha),
        grid_spec=pltpu.PrefetchScalarGridSpec(
            num_scalar_prefetch=1,
            grid=(bsz, n_t),
            in_specs=[
                pl.BlockSpec(memory_space=pl.ANY),
                pl.BlockSpec((None, TOKEN_BLOCK, LANES), blk),
                pl.BlockSpec((None, TOKEN_BLOCK, d), tok),
                pl.BlockSpec((None, 6, d), mod_row),
                pl.BlockSpec(lng.shape, const2),
                pl.BlockSpec(lnb.shape, const2),
            ],
            out_specs=pl.BlockSpec((None, TOKEN_BLOCK, d), tok),
            scratch_shapes=[pltpu.VMEM((2, 2, TOKEN_BLOCK * d // LANES, LANES), F32),
                            pltpu.SemaphoreType.DMA((2, 2))],
        ),
        out_shape=jax.ShapeDtypeStruct(x.shape, F32),
        compiler_params=_params(2),
        name="moe_combine_ln",
    )(dest, ys, wcol, x, mod, lng, lnb)


def _qkv_columns(kind):
    if kind == 0:
        perm, *_ = _slot_layout(A_HEAD_DIM // 4)
        nq, nk = A_HEADS, A_KV_HEADS
        q = np.concatenate([s * LANES + perm for s in range(nq)])
        k = np.concatenate([nq * LANES + s * LANES + perm for s in range(nk)])
        v = np.arange((nq + nk) * LANES, (nq + 2 * nk) * LANES)
        return np.concatenate([q, k, v]), nq, nk, nk
    perm, axis_of, freq_of, half_of, _ = _slot_layout(B_HEAD_DIM // 4)
    nf = B_HEAD_DIM // 4
    if kind == 1:
        nq = B_HEADS // 2
        q = np.concatenate([s * LANES + perm for s in range(nq)])
        k_base = B_HEADS * B_HEAD_DIM
        v_base = k_base + B_KV_HEADS * B_HEAD_DIM
        in_head = axis_of * 2 * nf + half_of * nf + freq_of
        k = np.concatenate([k_base + h * B_HEAD_DIM + in_head for h in range(B_KV_HEADS)])
        v = np.concatenate([v_base + h * B_HEAD_DIM + np.arange(LANES) % B_HEAD_DIM for h in range(B_KV_HEADS)])
        return np.concatenate([q, k, v]), nq, B_KV_HEADS, B_KV_HEADS
    nq = C_HEADS
    q = np.concatenate([s * LANES + perm for s in range(nq)])
    k = np.concatenate([nq * LANES + s * LANES + perm for s in range(nq)])
    v = np.arange(2 * nq * LANES, 3 * nq * LANES)
    return np.concatenate([q, k, v]), nq, nq, nq


def kernel(x, c, ctx, c_ctx, w_mod, b_mod, ln_gain, ln_bias, a_w_qkv, a_q_gain, a_k_gain, a_w_o, b_w_qkv, b_sink, b_w_o, c_w_qkv, c_lam_q1, c_lam_k1, c_lam_q2, c_lam_k2, c_subln_gain, c_w_o, moe_w_group, moe_b_group, moe_w_expert, moe_b_expert, moe_w_gate, moe_w_up, moe_w_down):
    bsz, seq, d = x.shape
    n_ctx = ctx.shape[1]
    depth = w_mod.shape[0]
    assert n_ctx == TOKEN_BLOCK and seq % TOKEN_BLOCK == 0 and seq >= TOKEN_BLOCK + 2 * WINDOW
    assert d % (SUBLANES * LANES) == 0
    t_len = n_ctx + seq
    n_tok = bsz * t_len
    alpha = (2.0 * depth) ** 0.25

    mod_rows = -(-(bsz + 1) // SUBLANES) * SUBLANES
    cc = jnp.concatenate([c, c_ctx[None, :], jnp.zeros((mod_rows - bsz - 1, d), F32)], axis=0)
    mod_all = _modulation(cc, w_mod, b_mod).reshape(depth, mod_rows, 6, d)

    stream = (ctx, x) if depth > 1 else jnp.concatenate([ctx, x], axis=1)
    tables = {hd: _rope_tables(seq, n_ctx, hd // 4, hd ** -0.5 * LOG2_E) for hd in (A_HEAD_DIM, B_HEAD_DIM)}
    perm_a = _slot_layout(A_HEAD_DIM // 4)[0]

    for i in range(depth):
        kind, j = i % N_MIXERS, i // N_MIXERS
        mod = mod_all[i]
        cols, n_q, n_k, n_v = _qkv_columns(kind)
        if kind == 0:
            w = a_w_qkv[j][:, cols].astype(BF16)
            gains = (a_q_gain[j][perm_a][None, :], a_k_gain[j][perm_a][None, :])
            q, k, v = _qkv_call(stream, mod, w, tables[A_HEAD_DIM], gains, n_q, n_k, n_v)
            o = _attn_a(q, k, v, n_ctx)
            w_o = a_w_o[j]
        elif kind == 1:
            w = b_w_qkv[j][:, cols].astype(BF16)
            q, k, v = _qkv_call(stream, mod, w, tables[B_HEAD_DIM], None, n_q, n_k, n_v)
            o = _attn_b(b_sink[j], q, k, v, n_ctx)
            w_o = b_w_o[j]
        else:
            w = c_w_qkv[j][:, cols].astype(BF16)
            q, k, v = _qkv_call(stream, mod, w, tables[C_HEAD_DIM], None, n_q, n_k, n_v)
            lam_init = 0.8 - 0.6 * float(np.exp(-0.3 * i))
            lam_vecs = jnp.stack([c_lam_q1[j], c_lam_k1[j], c_lam_q2[j], c_lam_k2[j]], axis=0)
            o = _attn_c(lam_vecs, c_subln_gain[j][None, :], q, k, v, n_ctx, lam_init)
            w_o = c_w_o[j]

        w_router = jnp.zeros((ROUTER_ROWS, d), F32)
        w_router = w_router.at[0:N_GROUPS].set(moe_w_group[i].T).at[SUBLANES:SUBLANES + N_EXPERTS].set(moe_w_expert[i].T)
        b_router = jnp.zeros((ROUTER_ROWS, 1), F32)
        b_router = b_router.at[0:N_GROUPS, 0].set(moe_b_group[i]).at[SUBLANES:SUBLANES + N_EXPERTS, 0].set(moe_b_expert[i])

        last = i == depth - 1
        stream, h2, rinfo, wcol, counts = _oproj_router(
            o, w_o.astype(BF16), stream, mod, ln_gain[i], ln_bias[i], w_router, b_router, alpha, 1 if last else 0)

        n_tiles = -(-2 * rinfo.shape[0] * TOKEN_BLOCK // EXPERT_TILE) + N_EXPERTS
        xs_rows = n_tiles * EXPERT_TILE
        cnt = counts[:, 0].astype(jnp.int32)
        tiles = (cnt + EXPERT_TILE - 1) // EXPERT_TILE
        tile_end = jnp.cumsum(tiles)
        row_off = (tile_end - tiles) * EXPERT_TILE
        n_used = tile_end[-1:].astype(jnp.int32)
        tile_ids = jnp.arange(n_tiles, dtype=jnp.int32)
        tile_expert = jnp.minimum(
            jnp.sum((tile_end[None, :] <= tile_ids[:, None]).astype(jnp.int32), axis=1), N_EXPERTS - 1)

        ids = jnp.arange(N_EXPERTS, dtype=jnp.int32)
        later_used = jnp.logical_and(ids[None, :] > ids[:, None], tiles[None, :] > 0)
        next_expert = jnp.min(jnp.where(later_used, ids[None, :], N_EXPERTS), axis=1)
        next_expert = jnp.where(next_expert == N_EXPERTS, -1, next_expert).astype(jnp.int32)

        expert_start = jnp.sum(jnp.where(rinfo[:, 0:2, :, None] == ids, row_off, 0), axis=-1)
        dest = (expert_start + rinfo[:, 2:4, :]).reshape(-1)

        xs = _dispatch(row_off, tiles, dest, h2, xs_rows, d // LANES)
        ys = _experts(tile_expert, n_used, next_expert, xs, moe_w_gate, moe_w_up, moe_w_down, i)
        stream = _combine(dest, ys, wcol, stream, mod, ln_gain[i], ln_bias[i], alpha, not last)

    return stream
```
